```python
import jax, jax.numpy as jnp
from jax import lax
import numpy as np

D_MODEL = 1024
BATCH = 2
SEQ = 8192
DEPTH = 2
DEC_BATCH = 128
DEC_SEQ = 4
PAST_LEN = 8192
PAGE_SIZE = 128

HEAD_DIM = 64
FOX_HEADS = 8
FOX_KV_HEADS = 2
SB_HEADS = 8
SB_KV_HEADS = 2
MLA_HEADS = 8
MLA_NOPE_DIM = 64
MLA_ROPE_DIM = 32
MLA_V_DIM = 64
MLA_Q_LORA = 256
MLA_KV_LORA = 128
ROPE_THETA = 10000.0
N_EXPERTS = 16
N_GROUPS = 4
EXPERTS_PER_GROUP = N_EXPERTS // N_GROUPS
TOP_GROUPS = 1
TOP_K = 2
D_EXPERT = 512
Q_BLOCK = 128
NORM_EPS = 1e-6
FORGET_BIAS_INIT = 3.0

FOX_Q_W = FOX_HEADS * HEAD_DIM
FOX_KV_W = FOX_KV_HEADS * HEAD_DIM
SB_Q_W = SB_HEADS * HEAD_DIM
SB_KV_W = SB_KV_HEADS * HEAD_DIM
MLA_QK_DIM = MLA_NOPE_DIM + MLA_ROPE_DIM
MLA_O_W = MLA_HEADS * MLA_V_DIM
IN_SPLITS = (FOX_Q_W, FOX_KV_W, FOX_KV_W, FOX_HEADS,
             SB_Q_W, SB_KV_W, SB_KV_W,
             MLA_Q_LORA, MLA_KV_LORA + MLA_ROPE_DIM,
             D_MODEL, D_MODEL, D_MODEL)
IN_WIDTH = sum(IN_SPLITS)

kernel_name = "hybrid_fox_stickbreak_mla_moe_step"


def split_cols(a, widths):
    outs, start = [], 0
    for w in widths:
        outs.append(a[..., start:start + w])
        start += w
    return outs


def rms_norm(x, g):
    xf = x.astype(jnp.float32)
    y = xf * lax.rsqrt(jnp.mean(xf * xf, axis=-1, keepdims=True) + NORM_EPS)
    return (y * g.astype(jnp.float32)).astype(x.dtype)


def rope(x, pos):
    half = MLA_ROPE_DIM // 2
    inv = ROPE_THETA ** (-jnp.arange(half, dtype=jnp.float32) / half)
    ang = pos.astype(jnp.float32)[:, None] * inv[None, :]
    ang = ang.reshape(ang.shape[:1] + (1,) * (x.ndim - 3) + (half,))
    cos, sin = jnp.cos(ang), jnp.sin(ang)
    xf = x.astype(jnp.float32)
    x1, x2 = xf[..., :half], xf[..., half:]
    return jnp.concatenate([x1 * cos - x2 * sin, x1 * sin + x2 * cos], axis=-1).astype(x.dtype)


def fox_attend(q, k, v, fq, fk, qpos, kpos):
    s = jnp.einsum('bqhgd,bkhd->bhgqk', q, k).astype(jnp.float32) * (HEAD_DIM ** -0.5)
    decay = jnp.transpose(fq, (0, 2, 3, 1))[..., :, None] - jnp.transpose(fk, (0, 2, 3, 1))[..., None, :]
    mask = kpos[None, :] <= qpos[:, None]
    p = jax.nn.softmax(jnp.where(mask, s + decay, -jnp.inf), axis=-1)
    return jnp.einsum('bhgqk,bkhd->bqhgd', p.astype(v.dtype), v)


def sb_attend(q, k, v, qpos, kpos):
    z = jnp.einsum('bqhgd,bkhd->bhgqk', q, k).astype(jnp.float32) * (HEAD_DIM ** -0.5)
    valid = kpos[None, :] < qpos[:, None]
    log_1m = jnp.where(valid, jax.nn.log_sigmoid(-z), 0.0)
    later = lax.cumsum(log_1m, axis=z.ndim - 1, reverse=True) - log_1m
    a = jnp.where(valid, jnp.exp(jax.nn.log_sigmoid(z) + later), 0.0)
    return jnp.einsum('bhgqk,bkhd->bqhgd', a.astype(v.dtype), v)


def mla_attend(q_lat, q_rope, ckv, kr, qpos, kpos):
    s = (jnp.einsum('bqhl,bkl->bhqk', q_lat, ckv) + jnp.einsum('bqhr,bkr->bhqk', q_rope, kr)).astype(jnp.float32)
    s = s * (MLA_QK_DIM ** -0.5)
    mask = kpos[None, :] <= qpos[:, None]
    p = jax.nn.softmax(jnp.where(mask, s, -jnp.inf), axis=-1)
    return jnp.einsum('bhqk,bkl->bqhl', p.astype(ckv.dtype), ckv)


def sweep_query_blocks(fn, q_args, qpos):
    n_blk = qpos.shape[0] // Q_BLOCK

    def to_blocks(a):
        return jnp.moveaxis(a.reshape((a.shape[0], n_blk, Q_BLOCK) + a.shape[2:]), 1, 0)

    blocks = tuple(to_blocks(a) for a in q_args)
    out = lax.map(lambda t: fn(*t[0], t[1]), (blocks, qpos.reshape(n_blk, Q_BLOCK)))
    out = jnp.moveaxis(out, 0, 1)
    return out.reshape((out.shape[0], n_blk * Q_BLOCK) + out.shape[3:])


def gather_pages(cache, layer, page_table):
    pages = cache[layer, page_table]
    return pages.reshape((pages.shape[0], pages.shape[1] * pages.shape[2]) + pages.shape[3:])


def mixer_sublayer(h, pos, past, w_in, b_forget, g_mla_q, w_mla_uq, g_mla_kv, w_mla_uk, w_mla_uv,
                   w_fox_o, w_sb_o, w_mla_o, w_o):
    B, S, _ = h.shape
    proj = h @ w_in
    (fq_, fk_, fv_, ff_, sq_, sk_, sv_, mdq, mdkv, ga, gb, gc) = split_cols(proj, IN_SPLITS)
    g_fox = FOX_HEADS // FOX_KV_HEADS
    g_sb = SB_HEADS // SB_KV_HEADS
    fox_q = fq_.reshape(B, S, FOX_KV_HEADS, g_fox, HEAD_DIM)
    fox_k = fk_.reshape(B, S, FOX_KV_HEADS, HEAD_DIM)
    fox_v = fv_.reshape(B, S, FOX_KV_HEADS, HEAD_DIM)
    fox_logf = jax.nn.log_sigmoid(ff_.astype(jnp.float32) + b_forget.astype(jnp.float32))
    sb_q = sq_.reshape(B, S, SB_KV_HEADS, g_sb, HEAD_DIM)
    sb_k = sk_.reshape(B, S, SB_KV_HEADS, HEAD_DIM)
    sb_v = sv_.reshape(B, S, SB_KV_HEADS, HEAD_DIM)
    cq = rms_norm(mdq, g_mla_q)
    mq = (cq @ w_mla_uq).reshape(B, S, MLA_HEADS, MLA_QK_DIM)
    q_lat = jnp.einsum('bshn,lhn->bshl', mq[..., :MLA_NOPE_DIM], w_mla_uk)
    q_rope = rope(mq[..., MLA_NOPE_DIM:], pos)
    ckv = rms_norm(mdkv[..., :MLA_KV_LORA], g_mla_kv)
    kr = rope(mdkv[..., MLA_KV_LORA:], pos)
    new_rows = (fox_k, fox_v, fox_logf, sb_k, sb_v, ckv, kr)

    if past is None:
        keys = new_rows
        kpos = pos
        fox_F = jnp.cumsum(fox_logf, axis=1)
        fq_F = fox_F
    else:
        keys = tuple(jnp.concatenate([p.astype(n.dtype), n], axis=1) for p, n in zip(past, new_rows))
        kpos = jnp.arange(keys[0].shape[1])
        fox_F = jnp.cumsum(keys[2].astype(jnp.float32), axis=1)
        fq_F = fox_F[:, -S:]
    k_f, v_f, _, k_s, v_s, ckv_all, kr_all = keys
    K = k_f.shape[1]
    fq_g = fq_F.reshape(B, S, FOX_KV_HEADS, g_fox)
    fk_g = fox_F.reshape(B, K, FOX_KV_HEADS, g_fox)

    fox_fn = lambda q, f, qp: fox_attend(q, k_f, v_f, f, fk_g, qp, kpos)
    sb_fn = lambda q, qp: sb_attend(q, k_s, v_s, qp, kpos)
    mla_fn = lambda ql, qr, qp: mla_attend(ql, qr, ckv_all, kr_all, qp, kpos)
    if past is None:
        o_fox = sweep_query_blocks(fox_fn, (fox_q, fq_g), pos)
        o_sb = sweep_query_blocks(sb_fn, (sb_q,), pos)
        o_lat = sweep_query_blocks(mla_fn, (q_lat, q_rope), pos)
    else:
        o_fox = fox_fn(fox_q, fq_g, pos)
        o_sb = sb_fn(sb_q, pos)
        o_lat = mla_fn(q_lat, q_rope, pos)
    o_mla = jnp.einsum('bshl,lhv->bshv', o_lat, w_mla_uv)

    y_fox = o_fox.reshape(B, S, FOX_Q_W) @ w_fox_o
    y_sb = o_sb.reshape(B, S, SB_Q_W) @ w_sb_o
    y_mla = o_mla.reshape(B, S, MLA_O_W) @ w_mla_o
    merged = jax.nn.sigmoid(ga) * y_fox + jax.nn.sigmoid(gb) * y_sb + jax.nn.sigmoid(gc) * y_mla
    return merged @ w_o, new_rows


def moe_ffn(h, w_router, b_router, w_gate, w_up, w_down):
    scores = jax.nn.sigmoid(jnp.einsum('bsd,de->bse', h, w_router).astype(jnp.float32))
    sel = scores + b_router.astype(jnp.float32)
    grouped = sel.reshape(sel.shape[:-1] + (N_GROUPS, EXPERTS_PER_GROUP))
    group_score = jnp.sum(lax.top_k(grouped, TOP_K)[0], axis=-1)
    _, g_idx = lax.top_k(group_score, TOP_GROUPS)
    group_mask = jnp.sum(jax.nn.one_hot(g_idx, N_GROUPS, dtype=jnp.float32), axis=-2) > 0
    expert_mask = jnp.repeat(group_mask, EXPERTS_PER_GROUP, axis=-1)
    _, e_idx = lax.top_k(jnp.where(expert_mask, sel, -jnp.inf), TOP_K)
    w = jnp.take_along_axis(scores, e_idx, axis=-1)
    w = w / jnp.sum(w, axis=-1, keepdims=True)
    combine = jnp.sum(jax.nn.one_hot(e_idx, N_EXPERTS, dtype=jnp.float32) * w[..., None], axis=-2)
    y = jnp.zeros(h.shape, jnp.float32)
    for e in range(N_EXPERTS):
        hid = jax.nn.silu(h @ w_gate[e]) * (h @ w_up[e])
        y = y + combine[..., e:e + 1] * (hid @ w_down[e]).astype(jnp.float32)
    return y.astype(h.dtype)


def setup_inputs(seed: int = 0) -> dict:
    key = jax.random.key(seed)
    ks = iter(jax.random.split(key, 48))
    n_pages = PAST_LEN // PAGE_SIZE
    n_pool = DEC_BATCH * n_pages * 5 // 4

    def nrm(shape, scale):
        return jax.random.normal(next(ks), shape, jnp.float32) * scale

    d = D_MODEL
    inp = {}
    inp['x_prompt'] = nrm((BATCH, SEQ, d), 1.0)
    inp['x_sample'] = nrm((DEC_BATCH, DEC_SEQ, d), 1.0)
    inp['cache_fox_k'] = nrm((DEPTH, n_pool, PAGE_SIZE, FOX_KV_HEADS, HEAD_DIM), 1.0)
    inp['cache_fox_v'] = nrm((DEPTH, n_pool, PAGE_SIZE, FOX_KV_HEADS, HEAD_DIM), 1.0)
    inp['cache_fox_logf'] = jax.nn.log_sigmoid(nrm((DEPTH, n_pool, PAGE_SIZE, FOX_HEADS), 1.0) + FORGET_BIAS_INIT)
    inp['cache_sb_k'] = nrm((DEPTH, n_pool, PAGE_SIZE, SB_KV_HEADS, HEAD_DIM), 1.0)
    inp['cache_sb_v'] = nrm((DEPTH, n_pool, PAGE_SIZE, SB_KV_HEADS, HEAD_DIM), 1.0)
    inp['cache_mla_ckv'] = nrm((DEPTH, n_pool, PAGE_SIZE, MLA_KV_LORA), 1.0)
    inp['cache_mla_kr'] = nrm((DEPTH, n_pool, PAGE_SIZE, MLA_ROPE_DIM), 1.0)
    perm = jax.random.permutation(next(ks), n_pool)
    inp['page_table'] = perm[: DEC_BATCH * n_pages].reshape(DEC_BATCH, n_pages).astype(jnp.int32)
    inp['c_prompt'] = nrm((BATCH, d), 1.0)
    inp['c_sample'] = nrm((DEC_BATCH, d), 1.0)
    inp['w_ada'] = nrm((DEPTH, d, 6 * d), 0.5 * d ** -0.5)
    inp['b_ada'] = nrm((DEPTH, 6 * d), 0.02)
    inp['g_mix'] = 1.0 + nrm((DEPTH, d), 0.05)
    inp['g_ffn'] = 1.0 + nrm((DEPTH, d), 0.05)
    inp['g_final'] = 1.0 + nrm((d,), 0.05)
    inp['w_in'] = nrm((DEPTH, d, IN_WIDTH), d ** -0.5)
    inp['b_forget'] = FORGET_BIAS_INIT + nrm((DEPTH, FOX_HEADS), 0.5)
    inp['g_mla_q'] = 1.0 + nrm((DEPTH, MLA_Q_LORA), 0.05)
    inp['w_mla_uq'] = nrm((DEPTH, MLA_Q_LORA, MLA_HEADS * MLA_QK_DIM), MLA_Q_LORA ** -0.5)
    inp['g_mla_kv'] = 1.0 + nrm((DEPTH, MLA_KV_LORA), 0.05)
    inp['w_mla_uk'] = nrm((DEPTH, MLA_KV_LORA, MLA_HEADS, MLA_NOPE_DIM), MLA_KV_LORA ** -0.5)
    inp['w_mla_uv'] = nrm((DEPTH, MLA_KV_LORA, MLA_HEADS, MLA_V_DIM), MLA_KV_LORA ** -0.5)
    inp['w_fox_o'] = nrm((DEPTH, FOX_Q_W, d), FOX_Q_W ** -0.5)
    inp['w_sb_o'] = nrm((DEPTH, SB_Q_W, d), SB_Q_W ** -0.5)
    inp['w_mla_o'] = nrm((DEPTH, MLA_O_W, d), MLA_O_W ** -0.5)
    inp['w_o'] = nrm((DEPTH, d, d), d ** -0.5)
    inp['w_router'] = nrm((d, N_EXPERTS), d ** -0.5)
    inp['b_router'] = nrm((N_EXPERTS,), 0.01)
    inp['w_exp_gate'] = nrm((DEPTH, N_EXPERTS, d, D_EXPERT), d ** -0.5)
    inp['w_exp_up'] = nrm((DEPTH, N_EXPERTS, d, D_EXPERT), d ** -0.5)
    inp['w_exp_down'] = nrm((DEPTH, N_EXPERTS, D_EXPERT, d), D_EXPERT ** -0.5)
    return inp


def reference(x_prompt, x_sample, cache_fox_k, cache_fox_v, cache_fox_logf, cache_sb_k, cache_sb_v,
              cache_mla_ckv, cache_mla_kr, page_table, c_prompt, c_sample, w_ada, b_ada, g_mix, g_ffn,
              g_final, w_in, b_forget, g_mla_q, w_mla_uq, g_mla_kv, w_mla_uk, w_mla_uv, w_fox_o, w_sb_o,
              w_mla_o, w_o, w_router, b_router, w_exp_gate, w_exp_up, w_exp_down):
    caches = (cache_fox_k, cache_fox_v, cache_fox_logf, cache_sb_k, cache_sb_v, cache_mla_ckv, cache_mla_kr)

    def trunk(x, c, pos, paged):
        rows_per_cache = [[] for _ in caches]
        for l in range(DEPTH):
            mod = jnp.einsum('bd,de->be', jax.nn.silu(c), w_ada[l]) + b_ada[l]
            sh1, sc1, gt1, sh2, sc2, gt2 = [m[:, None, :] for m in jnp.split(mod, 6, axis=-1)]
            h = rms_norm(x, g_mix[l]) * (1.0 + sc1) + sh1
            past = tuple(gather_pages(cc, l, page_table) for cc in caches) if paged else None
            mix_out, rows = mixer_sublayer(h, pos, past, w_in[l], b_forget[l], g_mla_q[l], w_mla_uq[l],
                                           g_mla_kv[l], w_mla_uk[l], w_mla_uv[l], w_fox_o[l], w_sb_o[l],
                                           w_mla_o[l], w_o[l])
            x = x + gt1 * mix_out
            h = rms_norm(x, g_ffn[l]) * (1.0 + sc2) + sh2
            x = x + gt2 * moe_ffn(h, w_router, b_router, w_exp_gate[l], w_exp_up[l], w_exp_down[l])
            for lst, r in zip(rows_per_cache, rows):
                lst.append(r)
        return rms_norm(x, g_final), [jnp.stack(lst) for lst in rows_per_cache]

    pos_p = jnp.arange(x_prompt.shape[1])
    y_prompt, new_p = trunk(x_prompt, c_prompt, pos_p, False)
    past_len = page_table.shape[1] * PAGE_SIZE
    pos_s = past_len + jnp.arange(x_sample.shape[1])
    y_sample, new_s = trunk(x_sample, c_sample, pos_s, True)
    p_fox_k, p_fox_v, p_fox_logf, p_sb_k, p_sb_v, p_mla_ckv, p_mla_kr = new_p
    s_fox_k, s_fox_v, s_fox_logf, s_sb_k, s_sb_v, s_mla_ckv, s_mla_kr = new_s
    return (y_prompt, y_sample, p_fox_k, p_fox_v, p_fox_logf, p_sb_k, p_sb_v, p_mla_ckv, p_mla_kr,
            s_fox_k, s_fox_v, s_fox_logf, s_sb_k, s_sb_v, s_mla_ckv, s_mla_kr)
```

```python
import functools

import jax
import jax.numpy as jnp
import numpy as np
from jax import lax
from jax.experimental import pallas as pl
from jax.experimental.pallas import tpu as pltpu

F32 = jnp.float32
BF = jnp.bfloat16

HEAD_DIM = 64
N_HEADS = 8
KV_HEADS = 2
GROUP = N_HEADS // KV_HEADS
MLA_NOPE = 64
MLA_ROPE = 32
MLA_QK = MLA_NOPE + MLA_ROPE
MLA_V = 64
MLA_Q_LORA = 256
MLA_KV_LORA = 128
ROPE_THETA = 10000.0
N_EXPERTS = 16
EXPERTS_PER_GROUP = 4
NORM_EPS = 1e-6
LANES = 128
NEG = -1e30

C_FQ, C_SQ, C_KV, C_GATE, C_MLA = 0, 1024, 2048, 3072, 6144
IN_W = 6656
M_KR, M_KRROT, M_FF = 0, 32, 64
A_FQ = 64
A_FK = 67

VMEM_LIMIT = 56 * 1024 * 1024


def _cparams(sem):
    return pltpu.CompilerParams(dimension_semantics=sem, vmem_limit_bytes=VMEM_LIMIT)


def _split3(x):
    hi = x.astype(BF)
    r1 = x - hi.astype(F32)
    mid = r1.astype(BF)
    lo = (r1 - mid.astype(F32)).astype(BF)
    return hi, mid, lo


def _softplus(z):
    return jnp.maximum(z, 0.0) + jnp.log1p(jnp.exp(-jnp.abs(z)))


def _ada_kernel(c_ref, w_ref, b_ref, o_ref):
    c = c_ref[...]
    a = (c * jax.nn.sigmoid(c)).astype(BF)
    o_ref[...] = jnp.dot(a, w_ref[...].astype(BF), preferred_element_type=F32) + b_ref[...]


def _ada(c_all, w_ada, b_ada):
    depth, d, n = w_ada.shape
    r = c_all.shape[0]
    tn = 1536
    return pl.pallas_call(
        _ada_kernel,
        grid=(depth, n // tn),
        in_specs=[pl.BlockSpec((r, d), lambda l, j: (0, 0)),
                  pl.BlockSpec((None, d, tn), lambda l, j: (l, 0, j)),
                  pl.BlockSpec((None, 1, tn), lambda l, j: (l, 0, j))],
        out_specs=pl.BlockSpec((None, r, tn), lambda l, j: (l, 0, j)),
        out_shape=jax.ShapeDtypeStruct((depth, r, n), F32),
        compiler_params=_cparams(("arbitrary", "arbitrary")),
        name="ada_mod",
    )(c_all, w_ada, b_ada.reshape(depth, 1, n))


def _inproj_kernel(x_ref, g_ref, sc_ref, sh_ref, w_ref, o_ref, h_ref):
    @pl.when(pl.program_id(1) == 0)
    def _():
        x = x_ref[...]
        y = x * lax.rsqrt(jnp.mean(x * x, axis=-1, keepdims=True) + NORM_EPS) * g_ref[...]
        h_ref[...] = (y * (1.0 + sc_ref[...]) + sh_ref[...]).astype(BF)

    o_ref[...] = jnp.dot(h_ref[...], w_ref[...], preferred_element_type=F32)


def _mod_spec(m, t, tm):
    nb, r, d = m.shape
    if r == 1:
        per = (t // tm) // nb
        return pl.BlockSpec((None, 1, d), lambda i, *_: (i // per, 0, 0))
    return pl.BlockSpec((None, tm, d), lambda i, *_: (0, i, 0))


def _inproj(x, g, sc, sh, w, tm):
    t, d = x.shape
    n = w.shape[1]
    tn = 512
    return pl.pallas_call(
        _inproj_kernel,
        grid=(t // tm, n // tn),
        in_specs=[pl.BlockSpec((tm, d), lambda i, j: (i, 0)),
                  pl.BlockSpec((1, d), lambda i, j: (0, 0)),
                  _mod_spec(sc, t, tm), _mod_spec(sh, t, tm),
                  pl.BlockSpec((d, tn), lambda i, j: (0, j))],
        out_specs=pl.BlockSpec((tm, tn), lambda i, j: (i, j)),
        out_shape=jax.ShapeDtypeStruct((t, n), F32),
        scratch_shapes=[pltpu.VMEM((tm, d), BF)],
        compiler_params=_cparams(("arbitrary", "arbitrary")),
        name="in_proj",
    )(x, g.reshape(1, d), sc, sh, w)


def _post_kernel(qf_ref, qs_ref, kv_ref, ml_ref, cs_ref, bf_ref, gq_ref, gkv_ref,
                 wn_ref, wra_ref, wrb_ref, wuk_ref, sel_ref, cst_ref,
                 fq_o, sq_o, kvb_o, mq_o, mk_o, lf_o, ckv_o, kr_o, carry_ref, *, mla_scale):
    tm = qf_ref.shape[0]

    @pl.when(pl.program_id(1) == 0)
    def _():
        carry_ref[...] = jnp.zeros_like(carry_ref)

    lane = lax.broadcasted_iota(jnp.int32, (tm, LANES), 1)
    misc = ml_ref[:, MLA_Q_LORA + MLA_KV_LORA:]
    cs = cs_ref[...]

    t1 = misc * cs
    kr = jnp.where(lane < MLA_ROPE, t1 + pltpu.roll(t1, LANES - M_KRROT, 1), 0.0)
    kr_o[...] = kr[:, :MLA_ROPE]

    lf = pltpu.roll(misc + bf_ref[...], LANES - M_FF, 1)
    lf = jnp.where(lane < N_HEADS, -_softplus(-lf), 0.0)
    lf_o[...] = lf[:, :N_HEADS]

    row = lax.broadcasted_iota(jnp.int32, (tm, tm), 0)
    col = lax.broadcasted_iota(jnp.int32, (tm, tm), 1)
    tri = (col <= row).astype(BF)
    hi, mid, lo = _split3(lf)
    f_cum = (jnp.dot(tri, hi, preferred_element_type=F32)
             + jnp.dot(tri, mid, preferred_element_type=F32)
             + jnp.dot(tri, lo, preferred_element_type=F32)) + carry_ref[...]
    carry_ref[...] = f_cum[tm - 1:tm, :]
    fh, fm, fl = _split3(f_cum)
    ext = (jnp.dot(fh, sel_ref[0], preferred_element_type=F32)
           + jnp.dot(fm, sel_ref[1], preferred_element_type=F32)
           + jnp.dot(fl, sel_ref[2], preferred_element_type=F32)) + cst_ref[...]

    scale = HEAD_DIM ** -0.5
    fq_o[...] = (qf_ref[...] * scale + ext[:, :N_HEADS * LANES]).astype(BF)
    sq_o[...] = (qs_ref[...] * scale).astype(BF)
    kv = kv_ref[...]
    kvb_o[:, :KV_HEADS * LANES] = (kv[:, :KV_HEADS * LANES] + ext[:, N_HEADS * LANES:]).astype(BF)
    kvb_o[:, KV_HEADS * LANES:] = kv[:, KV_HEADS * LANES:].astype(BF)

    mdq = ml_ref[:, :MLA_Q_LORA]
    cq = (mdq * lax.rsqrt(jnp.mean(mdq * mdq, axis=-1, keepdims=True) + NORM_EPS) * gq_ref[...]).astype(BF)
    nope = jnp.dot(cq, wn_ref[...], preferred_element_type=F32).astype(BF)
    q_lat = jnp.dot(nope, wuk_ref[...], preferred_element_type=F32)
    ra = jnp.dot(cq, wra_ref[...], preferred_element_type=F32)
    rb = jnp.dot(cq, wrb_ref[...], preferred_element_type=F32)
    cos_t = jnp.where(lane < MLA_ROPE, cs, 0.0)
    sin_t = jnp.where(lane < MLA_ROPE, pltpu.roll(cs, LANES - MLA_ROPE, 1), 0.0)
    for h in range(N_HEADS):
        sl = slice(h * LANES, (h + 1) * LANES)
        mq_o[:, 2 * h * LANES:(2 * h + 1) * LANES] = (q_lat[:, sl] * mla_scale).astype(BF)
        mq_o[:, (2 * h + 1) * LANES:(2 * h + 2) * LANES] = (
            (ra[:, sl] * cos_t + rb[:, sl] * sin_t) * mla_scale).astype(BF)

    craw = ml_ref[:, MLA_Q_LORA:MLA_Q_LORA + MLA_KV_LORA]
    ckv = craw * lax.rsqrt(jnp.mean(craw * craw, axis=-1, keepdims=True) + NORM_EPS) * gkv_ref[...]
    ckv_o[...] = ckv
    mk_o[:, :LANES] = ckv.astype(BF)
    mk_o[:, LANES:] = kr.astype(BF)


def _post(proj, cs_tab, pw, nb, s, tm):
    p3 = proj.reshape(nb, s, IN_W)
    full = lambda a: pl.BlockSpec(a.shape, lambda b, i: (0,) * a.ndim)
    seg = lambda w, c: pl.BlockSpec((None, tm, w), lambda b, i: (b, i, c // w))
    row = lambda w: pl.BlockSpec((None, tm, w), lambda b, i: (b, i, 0))
    outs = [(N_HEADS * LANES, BF), (N_HEADS * LANES, BF), (4 * KV_HEADS * LANES, BF),
            (N_HEADS * 2 * LANES, BF), (2 * LANES, BF), (N_HEADS, F32), (MLA_KV_LORA, F32), (MLA_ROPE, F32)]
    return pl.pallas_call(
        functools.partial(_post_kernel, mla_scale=MLA_QK ** -0.5),
        grid=(nb, s // tm),
        in_specs=[seg(1024, C_FQ), seg(1024, C_SQ), seg(1024, C_KV), seg(512, C_MLA),
                  pl.BlockSpec((tm, LANES), lambda b, i: (i, 0)),
                  full(pw["bf"]), full(pw["gq"]), full(pw["gkv"]), full(pw["wn"]), full(pw["wra"]),
                  full(pw["wrb"]), full(pw["wuk"]), full(pw["sel"]), full(pw["cst"])],
        out_specs=[row(w) for w, _ in outs],
        out_shape=[jax.ShapeDtypeStruct((nb, s, w), dt) for w, dt in outs],
        scratch_shapes=[pltpu.VMEM((1, LANES), F32)],
        compiler_params=_cparams(("arbitrary", "arbitrary")),
        name="post_proj",
    )(p3, p3, p3, p3, cs_tab, pw["bf"], pw["gq"], pw["gkv"], pw["wn"], pw["wra"], pw["wrb"],
      pw["wuk"], pw["sel"], pw["cst"])


def _stack_heads(q_ref, n, w):
    return jnp.concatenate([q_ref[:, g * w:(g + 1) * w] for g in range(n)], axis=0)


def _softmax_block(q, k, v, m_ref, l_ref, acc_ref, mask):
    s = lax.dot_general(q, k, (((1,), (1,)), ((), ())), preferred_element_type=F32)
    if mask is not None:
        s = jnp.where(mask, s, NEG)
    m_prev = m_ref[...]
    m_next = jnp.maximum(m_prev, jnp.max(s, axis=1)[:, None])
    p = jnp.exp(s - jnp.tile(m_next, (1, s.shape[1] // LANES)))
    alpha = jnp.exp(m_prev - m_next)
    l_ref[...] = alpha * l_ref[...] + jnp.sum(p, axis=1)[:, None]
    m_ref[...] = m_next
    acc_ref[...] = acc_ref[...] * alpha + jnp.dot(p.astype(BF), v, preferred_element_type=F32)


def _causal_mask(i, tq, j0, tk, n_rep, strict):
    qpos = i * tq + lax.broadcasted_iota(jnp.int32, (tq, tk), 0)
    kpos = j0 + lax.broadcasted_iota(jnp.int32, (tq, tk), 1)
    m = (kpos < qpos) if strict else (kpos <= qpos)
    return jnp.tile(m, (n_rep, 1))


def _fox_kernel(q_ref, k_ref, v_ref, o_ref, m_ref, l_ref, acc_ref, *, tq, tk):
    i = pl.program_id(2)
    q = _stack_heads(q_ref, GROUP, LANES)
    m_ref[...] = jnp.full_like(m_ref, NEG)
    l_ref[...] = jnp.zeros_like(l_ref)
    acc_ref[...] = jnp.zeros_like(acc_ref)
    n_full = (i * tq) // tk

    def body(j, c):
        off = pl.multiple_of(j * tk, tk)
        _softmax_block(q, k_ref[pl.ds(off, tk), :], v_ref[pl.ds(off, tk), :], m_ref, l_ref, acc_ref, None)
        return c

    lax.fori_loop(0, n_full, body, 0)
    off = pl.multiple_of(n_full * tk, tk)
    _softmax_block(q, k_ref[pl.ds(off, tk), :], v_ref[pl.ds(off, tk), :], m_ref, l_ref, acc_ref,
                   _causal_mask(i, tq, off, tk, GROUP, False))
    out = acc_ref[...] / l_ref[...]
    for g in range(GROUP):
        o_ref[:, g * LANES:(g + 1) * LANES] = out[g * tq:(g + 1) * tq].astype(o_ref.dtype)


def _fox_attn(fq, kvb, tq, tk):
    b, s, _ = fq.shape
    m = GROUP * tq
    return pl.pallas_call(
        functools.partial(_fox_kernel, tq=tq, tk=tk),
        grid=(b, KV_HEADS, s // tq),
        in_specs=[pl.BlockSpec((None, tq, GROUP * LANES), lambda bb, h, i: (bb, i, h)),
                  pl.BlockSpec((None, s, LANES), lambda bb, h, i: (bb, 0, h)),
                  pl.BlockSpec((None, s, LANES), lambda bb, h, i: (bb, 0, KV_HEADS + h))],
        out_specs=pl.BlockSpec((None, tq, GROUP * LANES), lambda bb, h, i: (bb, i, h)),
        out_shape=jax.ShapeDtypeStruct((b, s, N_HEADS * LANES), BF),
        scratch_shapes=[pltpu.VMEM((m, LANES), F32), pltpu.VMEM((m, LANES), F32), pltpu.VMEM((m, LANES), F32)],
        compiler_params=_cparams(("arbitrary", "arbitrary", "arbitrary")),
        name="fox_attn",
    )(fq, kvb, kvb)


def _suffix_in_block(x, upper, pieces=3):
    parts = _split3(x)[:pieces]
    r = x.shape[0]
    res = jnp.dot(jnp.concatenate(parts, axis=0), upper, preferred_element_type=F32)
    out = res[:r]
    for p in range(1, pieces):
        out = out + res[p * r:(p + 1) * r]
    return out


def _sb_block(q, k, v, upper, r_ref, acc_ref, mask):
    z = lax.dot_general(q, k, (((1,), (1,)), ((), ())), preferred_element_type=F32)
    sp = _softplus(z)
    l1m = -sp if mask is None else jnp.where(mask, -sp, 0.0)
    later = _suffix_in_block(l1m, upper, pieces=2)
    e = z - sp + later + jnp.tile(r_ref[...], (1, z.shape[1] // LANES))
    a = jnp.exp(e) if mask is None else jnp.where(mask, jnp.exp(e), 0.0)
    acc_ref[...] += jnp.dot(a.astype(BF), v, preferred_element_type=F32)
    r_ref[...] += jnp.sum(l1m, axis=1)[:, None]


def _sb_kernel(q_ref, k_ref, v_ref, u_ref, o_ref, r_ref, acc_ref, *, tq):
    i = pl.program_id(2)
    q = _stack_heads(q_ref, GROUP, LANES)
    r_ref[...] = jnp.zeros_like(r_ref)
    acc_ref[...] = jnp.zeros_like(acc_ref)
    upper = u_ref[...]
    off = pl.multiple_of(i * tq, tq)
    _sb_block(q, k_ref[pl.ds(off, tq), :], v_ref[pl.ds(off, tq), :], upper, r_ref, acc_ref,
              _causal_mask(i, tq, off, tq, GROUP, True))

    def body(n, c):
        o2 = pl.multiple_of((i - 1 - n) * tq, tq)
        _sb_block(q, k_ref[pl.ds(o2, tq), :], v_ref[pl.ds(o2, tq), :], upper, r_ref, acc_ref, None)
        return c

    lax.fori_loop(0, i, body, 0)
    out = acc_ref[...]
    for g in range(GROUP):
        o_ref[:, g * LANES:(g + 1) * LANES] = out[g * tq:(g + 1) * tq].astype(o_ref.dtype)


def _upper(w):
    j = lax.broadcasted_iota(jnp.int32, (w, w), 0)
    s = lax.broadcasted_iota(jnp.int32, (w, w), 1)
    return (j > s).astype(BF)


def _sb_attn(sq, kvb, tq):
    b, s, _ = sq.shape
    m = GROUP * tq
    return pl.pallas_call(
        functools.partial(_sb_kernel, tq=tq),
        grid=(b, KV_HEADS, s // tq),
        in_specs=[pl.BlockSpec((None, tq, GROUP * LANES), lambda bb, h, i: (bb, i, h)),
                  pl.BlockSpec((None, s, LANES), lambda bb, h, i: (bb, 0, 2 * KV_HEADS + h)),
                  pl.BlockSpec((None, s, LANES), lambda bb, h, i: (bb, 0, 3 * KV_HEADS + h)),
                  pl.BlockSpec((tq, tq), lambda bb, h, i: (0, 0))],
        out_specs=pl.BlockSpec((None, tq, GROUP * LANES), lambda bb, h, i: (bb, i, h)),
        out_shape=jax.ShapeDtypeStruct((b, s, N_HEADS * LANES), BF),
        scratch_shapes=[pltpu.VMEM((m, LANES), F32), pltpu.VMEM((m, LANES), F32)],
        compiler_params=_cparams(("arbitrary", "arbitrary", "arbitrary")),
        name="sb_attn",
    )(sq, kvb, kvb, _upper(tq))


def _mla_kernel(q_ref, k_ref, wuv_ref, o_ref, m_ref, l_ref, acc_ref, *, tq, tk):
    i = pl.program_id(1)
    q = _stack_heads(q_ref, N_HEADS, 2 * LANES)
    m_ref[...] = jnp.full_like(m_ref, NEG)
    l_ref[...] = jnp.zeros_like(l_ref)
    acc_ref[...] = jnp.zeros_like(acc_ref)
    n_full = (i * tq) // tk

    def body(j, c):
        off = pl.multiple_of(j * tk, tk)
        kk = k_ref[pl.ds(off, tk), :]
        _softmax_block(q, kk, kk[:, :LANES], m_ref, l_ref, acc_ref, None)
        return c

    lax.fori_loop(0, n_full, body, 0)
    off = pl.multiple_of(n_full * tk, tk)
    kk = k_ref[pl.ds(off, tk), :]
    _softmax_block(q, kk, kk[:, :LANES], m_ref, l_ref, acc_ref, _causal_mask(i, tq, off, tk, N_HEADS, False))
    o_lat = (acc_ref[...] / l_ref[...]).astype(BF)
    for h in range(N_HEADS):
        o_ref[:, h * LANES:(h + 1) * LANES] = jnp.dot(
            o_lat[h * tq:(h + 1) * tq], wuv_ref[h], preferred_element_type=F32).astype(o_ref.dtype)


def _mla_attn(mq, mk, wuv, tq, tk):
    b, s, _ = mq.shape
    m = N_HEADS * tq
    return pl.pallas_call(
        functools.partial(_mla_kernel, tq=tq, tk=tk),
        grid=(b, s // tq),
        in_specs=[pl.BlockSpec((None, tq, N_HEADS * 2 * LANES), lambda bb, i: (bb, i, 0)),
                  pl.BlockSpec((None, s, 2 * LANES), lambda bb, i: (bb, 0, 0)),
                  pl.BlockSpec(wuv.shape, lambda bb, i: (0, 0, 0))],
        out_specs=pl.BlockSpec((None, tq, N_HEADS * LANES), lambda bb, i: (bb, i, 0)),
        out_shape=jax.ShapeDtypeStruct((b, s, N_HEADS * LANES), BF),
        scratch_shapes=[pltpu.VMEM((m, LANES), F32), pltpu.VMEM((m, LANES), F32), pltpu.VMEM((m, LANES), F32)],
        compiler_params=_cparams(("arbitrary", "arbitrary")),
        name="mla_attn",
    )(mq, mk, wuv)


def _suffix_lanes(x, upper, bw):
    w = x.shape[1]
    carry = jnp.zeros((x.shape[0], 1), F32)
    outs = [None] * (w // bw)
    for b in reversed(range(w // bw)):
        xb = x[:, b * bw:(b + 1) * bw]
        outs[b] = _suffix_in_block(xb, upper) + carry
        carry = carry + jnp.sum(xb, axis=1, keepdims=True)
    return (outs[0] if len(outs) == 1 else jnp.concatenate(outs, axis=1)), carry


def _online(s, v_nt, v_n, m_ref, l_ref, acc_ref):
    m_prev = m_ref[...]
    m_next = jnp.maximum(m_prev, jnp.max(s, axis=1, keepdims=True))
    p = jnp.exp(s - m_next)
    alpha = jnp.exp(m_prev - m_next)
    l_ref[...] = alpha * l_ref[...] + jnp.sum(p, axis=1, keepdims=True)
    m_ref[...] = m_next
    pb = p.astype(BF)
    if v_nt is not None:
        pv = lax.dot_general(pb, v_nt, (((1,), (1,)), ((), ())), preferred_element_type=F32)
    else:
        pv = jnp.dot(pb, v_n, preferred_element_type=F32)
    acc_ref[...] = acc_ref[...] * alpha + pv


def _decode_block(qf, qs, ql, qr, fk, fv, sk, sv, ck, kr, lf, upper, bw, st, new_mask, cn):
    (fm, fl, fa, sr, sa, mm, mlr, ma, rf) = st
    nt = qf.shape[0] // N_HEADS
    w = fk.shape[1]
    s = jnp.dot(qf, fk.astype(BF), preferred_element_type=F32)
    if new_mask is None:
        later, tot = _suffix_lanes(lf, upper, bw)
        bias = later + rf[...]
        s = jnp.concatenate([s[t * N_HEADS:(t + 1) * N_HEADS] + (bias + cn[:, t:t + 1]) for t in range(nt)], axis=0)
        rf[...] += tot
    else:
        le, strict = new_mask
        s = jnp.concatenate([s[t * N_HEADS:(t + 1) * N_HEADS] + (cn[:, t:t + 1] - cn) for t in range(nt)], axis=0)
        s = jnp.where(le, s, NEG)
    _online(s, fv.astype(BF), None, fm, fl, fa)
    z = jnp.dot(qs, sk.astype(BF), preferred_element_type=F32)
    sp = _softplus(z)
    l1m = -sp if new_mask is None else jnp.where(new_mask[1], -sp, 0.0)
    later, tot = _suffix_lanes(l1m, upper, bw)
    e = jnp.exp(z - sp + later + sr[...])
    a = e if new_mask is None else jnp.where(new_mask[1], e, 0.0)
    sa[...] += lax.dot_general(a.astype(BF), sv.astype(BF), (((1,), (1,)), ((), ())), preferred_element_type=F32)
    sr[...] += tot
    ckb = ck.astype(BF)
    s = (lax.dot_general(ql, ckb, (((1,), (1,)), ((), ())), preferred_element_type=F32)
         + jnp.dot(qr, kr.astype(BF), preferred_element_type=F32))
    if new_mask is not None:
        s = jnp.where(new_mask[0], s, NEG)
    _online(s, None, ckb, mm, mlr, ma)


def _decode_kernel(pt_ref, qf_ref, qs_ref, ql_ref, qr_ref, fkn_ref, fvn_ref, skn_ref, svn_ref, ckn_ref,
                   krn_ref, lfn_ref, u_ref, wuv_ref,
                   fk_hbm, fv_hbm, sk_hbm, sv_hbm, ck_hbm, kr_hbm, lf_hbm,
                   of_ref, os_ref, om_ref,
                   fk_b, fv_b, sk_b, sv_b, ck_b, kr_b, lf_b, sem,
                   fm, fl, fa, sr, sa, mm, mlr, ma, rf, cn_ref, *, layer, ppc, nt):
    sq = pl.program_id(0)
    c = pl.program_id(1)
    n_seq = pl.num_programs(0)
    page = fk_b.shape[1]
    hbm = (fk_hbm, fv_hbm, sk_hbm, sv_hbm, ck_hbm, kr_hbm, lf_hbm)
    bufs = (fk_b, fv_b, sk_b, sv_b, ck_b, kr_b, lf_b)

    def copies(seq, first_page, slot):
        out = []
        for p in range(ppc):
            pg = pt_ref[seq, first_page + p]
            for a in range(7):
                if a == 4:
                    dst = bufs[a].at[slot, pl.ds(p * page, page), :]
                else:
                    dst = bufs[a].at[slot, :, pl.ds(p * page, page)]
                out.append(pltpu.make_async_copy(hbm[a].at[layer, pg], dst, sem.at[slot, a]))
        return out

    def start(seq, first_page, slot):
        for cp in copies(seq, first_page, slot):
            cp.start()

    def wait(slot):
        for cp in copies(0, 0, slot):
            cp.wait()

    @pl.when(jnp.logical_and(sq == 0, c == 0))
    def _():
        start(0, ppc, 0)

    qf, qs, ql, qr = qf_ref[...], qs_ref[...], ql_ref[...], qr_ref[...]
    upper = u_ref[...]
    st = (fm, fl, fa, sr, sa, mm, mlr, ma, rf)
    rows = nt * N_HEADS

    @pl.when(c == 0)
    def _():
        start(sq, 0, 1)
        for r in (fm, mm):
            r[...] = jnp.full_like(r, NEG)
        for r in (fl, fa, sr, sa, mlr, ma, rf):
            r[...] = jnp.zeros_like(r)
        lfn = lfn_ref[...]
        lane8 = lax.broadcasted_iota(jnp.int32, (N_HEADS, LANES), 1)
        later, tot = _suffix_lanes(jnp.where(lane8 < nt, lfn, 0.0), upper[:LANES, :LANES], LANES)
        cn = tot - later
        cn_ref[...] = cn
        trow = lax.broadcasted_iota(jnp.int32, (rows, LANES), 0) // N_HEADS
        kcol = lax.broadcasted_iota(jnp.int32, (rows, LANES), 1)
        _decode_block(qf, qs, ql, qr, fkn_ref[...], fvn_ref[...], skn_ref[...], svn_ref[...], ckn_ref[...],
                      krn_ref[...], None, upper[:LANES, :LANES], LANES, st, (kcol <= trow, kcol < trow), cn)
        wait(0)
        _decode_block(qf, qs, ql, qr, fk_b[0], fv_b[0], sk_b[0], sv_b[0], ck_b[0], kr_b[0], lf_b[0],
                      upper, upper.shape[0], st, None, cn)

    @pl.when(c == 1)
    def _():
        @pl.when(sq + 1 < n_seq)
        def _():
            start(sq + 1, ppc, 0)
        wait(1)
        _decode_block(qf, qs, ql, qr, fk_b[1], fv_b[1], sk_b[1], sv_b[1], ck_b[1], kr_b[1], lf_b[1],
                      upper, upper.shape[0], st, None, cn_ref[...])
        of_ref[...] = fa[...] / fl[...]
        os_ref[...] = sa[...]
        o_lat = (ma[...] / mlr[...]).astype(BF)
        om_ref[...] = jnp.dot(o_lat, wuv_ref[...], preferred_element_type=F32)


def _decode_attn(page_table, q4, new7, caches7, wuv_cat, layer, nt):
    qf, qs, ql, qr = q4
    db, rows, _ = qf.shape
    n_pages = page_table.shape[1]
    ppc = n_pages // 2
    page = caches7[0].shape[-1]
    w = ppc * page
    bw = min(256, w)
    upper = _upper(bw)
    per_seq = lambda a: pl.BlockSpec((None,) + a.shape[1:], lambda s, c, pt: (s,) + (0,) * (a.ndim - 1))
    full = lambda a: pl.BlockSpec(a.shape, lambda s, c, pt: (0,) * a.ndim)
    anyspec = pl.BlockSpec(memory_space=pl.ANY)
    out_w = (LANES, LANES, wuv_cat.shape[1])
    small = lambda wd: pltpu.VMEM((rows, wd), F32)
    grid_spec = pltpu.PrefetchScalarGridSpec(
        num_scalar_prefetch=1,
        grid=(db, 2),
        in_specs=[per_seq(a) for a in (qf, qs, ql, qr) + tuple(new7)] + [full(upper), full(wuv_cat)] + [anyspec] * 7,
        out_specs=[pl.BlockSpec((None, rows, wd), lambda s, c, pt: (s, 0, 0)) for wd in out_w],
        scratch_shapes=[pltpu.VMEM((2, LANES, w), F32)] * 4
        + [pltpu.VMEM((2, w, LANES), F32), pltpu.VMEM((2, MLA_ROPE, w), F32), pltpu.VMEM((2, N_HEADS, w), F32),
           pltpu.SemaphoreType.DMA((2, 7)),
           small(1), small(1), small(LANES), small(1), small(LANES), small(1), small(1), small(LANES),
           pltpu.VMEM((N_HEADS, 1), F32), pltpu.VMEM((N_HEADS, LANES), F32)],
    )
    return pl.pallas_call(
        functools.partial(_decode_kernel, layer=layer, ppc=ppc, nt=nt),
        grid_spec=grid_spec,
        out_shape=[jax.ShapeDtypeStruct((db, rows, wd), F32) for wd in out_w],
        compiler_params=_cparams(("arbitrary", "arbitrary")),
        name="decode_attn",
    )(page_table, qf, qs, ql, qr, *new7, upper, wuv_cat, *caches7)


def _outproj_kernel(of_ref, os_ref, om_ref, ga_ref, gb_ref, gc_ref, x_ref, gt_ref, sc_ref, sh_ref, g_ref,
                    wf_ref, ws_ref, wm_ref, wo_ref, xo_ref, ho_ref):
    yf = jnp.dot(of_ref[...], wf_ref[...], preferred_element_type=F32)
    ys = jnp.dot(os_ref[...], ws_ref[...], preferred_element_type=F32)
    ym = jnp.dot(om_ref[...], wm_ref[...], preferred_element_type=F32)
    merged = (jax.nn.sigmoid(ga_ref[...]) * yf + jax.nn.sigmoid(gb_ref[...]) * ys
              + jax.nn.sigmoid(gc_ref[...]) * ym)
    mix = jnp.dot(merged.astype(BF), wo_ref[...], preferred_element_type=F32)
    x = x_ref[...] + gt_ref[...] * mix
    xo_ref[...] = x
    y = x * lax.rsqrt(jnp.mean(x * x, axis=-1, keepdims=True) + NORM_EPS) * g_ref[...]
    ho_ref[...] = (y * (1.0 + sc_ref[...]) + sh_ref[...]).astype(BF)


def _outproj(o_f, o_s, o_m, proj, x, gt, sc, sh, g, wf, ws, wm, wo, tm):
    t, d = x.shape
    tok = lambda w: pl.BlockSpec((tm, w), lambda i: (i, 0))
    gate = lambda k: pl.BlockSpec((tm, d), lambda i: (i, C_GATE // d + k))
    mod = _mod_spec(gt, t, tm)
    full = lambda a: pl.BlockSpec(a.shape, lambda i: (0,) * a.ndim)
    return pl.pallas_call(
        _outproj_kernel,
        grid=(t // tm,),
        in_specs=[tok(o_f.shape[1]), tok(o_s.shape[1]), tok(o_m.shape[1]), gate(0), gate(1), gate(2), tok(d),
                  mod, mod, mod, full(g), full(wf), full(ws), full(wm), full(wo)],
        out_specs=[tok(d), tok(d)],
        out_shape=[jax.ShapeDtypeStruct((t, d), F32), jax.ShapeDtypeStruct((t, d), BF)],
        compiler_params=_cparams(("arbitrary",)),
        name="out_proj",
    )(o_f, o_s, o_m, proj, proj, proj, x, gt, sc, sh, g, wf, ws, wm, wo)


def _route(logits, bias):
    tm = logits.shape[0]
    lane = lax.broadcasted_iota(jnp.int32, (tm, LANES), 1)
    scores = jax.nn.sigmoid(logits)
    sel = scores + bias
    pos = lane % EXPERTS_PER_GROUP
    grp = lane // EXPERTS_PER_GROUP
    n_groups = N_EXPERTS // EXPERTS_PER_GROUP

    def shifted(x, k):
        return pltpu.roll(x, (LANES - k) % LANES, 1)

    def beats(other, mine, k):
        return (other > mine) | ((other == mine) & (k < 0))

    rank = jnp.zeros((tm, LANES), jnp.int32)
    for k in range(-EXPERTS_PER_GROUP + 1, EXPERTS_PER_GROUP):
        if k == 0:
            continue
        same = (pos + k >= 0) & (pos + k < EXPERTS_PER_GROUP)
        rank += (same & beats(shifted(sel, k), sel, k)).astype(jnp.int32)
    top2 = jnp.where(rank < 2, sel, 0.0)
    gscore = top2
    for k in range(-EXPERTS_PER_GROUP + 1, EXPERTS_PER_GROUP):
        if k == 0:
            continue
        same = (pos + k >= 0) & (pos + k < EXPERTS_PER_GROUP)
        gscore += jnp.where(same, shifted(top2, k), 0.0)
    grank = jnp.zeros((tm, LANES), jnp.int32)
    for k in range(-n_groups + 1, n_groups):
        if k == 0:
            continue
        ok = (grp + k >= 0) & (grp + k < n_groups)
        grank += (ok & beats(shifted(gscore, k * EXPERTS_PER_GROUP), gscore, k)).astype(jnp.int32)
    chosen = (grank == 0) & (rank < 2) & (lane < N_EXPERTS)
    wts = jnp.where(chosen, scores, 0.0)
    return wts / jnp.sum(wts, axis=1, keepdims=True)


def _moe_kernel(h_ref, x_ref, gt_ref, wr_ref, br_ref, wg_ref, wu_ref, wd_ref, gf_ref, o_ref,
                comb_ref, acc_ref, *, final):
    e = pl.program_id(1)
    h = h_ref[...]

    @pl.when(e == 0)
    def _():
        logits = (jnp.dot(h, wr_ref[0], preferred_element_type=F32)
                  + jnp.dot(h, wr_ref[1], preferred_element_type=F32))
        comb_ref[...] = _route(logits, br_ref[...])
        acc_ref[...] = jnp.zeros_like(acc_ref)

    a = jnp.dot(h, wg_ref[...], preferred_element_type=F32)
    u = jnp.dot(h, wu_ref[...], preferred_element_type=F32)
    hid = (a * jax.nn.sigmoid(a) * u).astype(BF)
    y = jnp.dot(hid, wd_ref[...], preferred_element_type=F32)
    lane = lax.broadcasted_iota(jnp.int32, comb_ref.shape, 1)
    ce = jnp.sum(jnp.where(lane == e, comb_ref[...], 0.0), axis=1, keepdims=True)
    acc_ref[...] += ce * y

    @pl.when(e == pl.num_programs(1) - 1)
    def _():
        x = x_ref[...] + gt_ref[...] * acc_ref[...]
        if final:
            x = x * lax.rsqrt(jnp.mean(x * x, axis=-1, keepdims=True) + NORM_EPS) * gf_ref[...]
        o_ref[...] = x


def _moe(h, x, gt, wr, br, wg, wu, wd, gf, tm, final):
    t, d = x.shape
    ne, _, de = wg.shape
    tok = pl.BlockSpec((tm, d), lambda i, e: (i, 0))
    full = lambda a: pl.BlockSpec(a.shape, lambda i, e: (0,) * a.ndim)
    return pl.pallas_call(
        functools.partial(_moe_kernel, final=final),
        grid=(t // tm, ne),
        in_specs=[tok, tok, _mod_spec(gt, t, tm), full(wr), full(br),
                  pl.BlockSpec((None, d, de), lambda i, e: (e, 0, 0)),
                  pl.BlockSpec((None, d, de), lambda i, e: (e, 0, 0)),
                  pl.BlockSpec((None, de, d), lambda i, e: (e, 0, 0)), full(gf)],
        out_specs=tok,
        out_shape=jax.ShapeDtypeStruct((t, d), F32),
        scratch_shapes=[pltpu.VMEM((tm, LANES), F32), pltpu.VMEM((tm, d), F32)],
        compiler_params=_cparams(("arbitrary", "arbitrary")),
        name="moe_ffn",
    )(h, x, gt, wr, br, wg, wu, wd, gf)


def _pad_heads(w, n_heads, width):
    lead = w.shape[:-1]
    w = w.reshape(lead + (n_heads, width))
    w = jnp.pad(w, [(0, 0)] * len(lead) + [(0, 0), (0, LANES - width)])
    return w.reshape(lead + (n_heads * LANES,))


def _rot_cols(w):
    half = MLA_ROPE // 2
    return jnp.concatenate([-w[..., half:], w[..., :half]], axis=-1)


def _in_weight(w):
    d = w.shape[0]
    o = 0
    parts = {}
    for name, width in (("fq", 512), ("fk", 128), ("fv", 128), ("ff", 8), ("sq", 512), ("sk", 128), ("sv", 128),
                        ("mdq", MLA_Q_LORA), ("mdkv", MLA_KV_LORA + MLA_ROPE), ("ga", d), ("gb", d), ("gc", d)):
        parts[name] = w[:, o:o + width]
        o += width
    kr = parts["mdkv"][:, MLA_KV_LORA:]
    misc = jnp.concatenate([kr, _rot_cols(kr), parts["ff"],
                            jnp.zeros((d, LANES - 2 * MLA_ROPE - N_HEADS), w.dtype)], axis=1)
    cols = [_pad_heads(parts["fq"], N_HEADS, HEAD_DIM), _pad_heads(parts["sq"], N_HEADS, HEAD_DIM),
            _pad_heads(parts["fk"], KV_HEADS, HEAD_DIM), _pad_heads(parts["fv"], KV_HEADS, HEAD_DIM),
            _pad_heads(parts["sk"], KV_HEADS, HEAD_DIM), _pad_heads(parts["sv"], KV_HEADS, HEAD_DIM),
            parts["ga"], parts["gb"], parts["gc"], parts["mdq"], parts["mdkv"][:, :MLA_KV_LORA], misc]
    return jnp.concatenate(cols, axis=1).astype(BF)


def _pad_rows(w, n_heads, width):
    d = w.shape[-1]
    w = w.reshape(n_heads, width, d)
    return jnp.pad(w, [(0, 0), (0, LANES - width), (0, 0)]).reshape(n_heads * LANES, d)


def _fox_select():
    sel = np.zeros((3, LANES, (N_HEADS + KV_HEADS) * LANES), np.float32)
    cst = np.zeros((1, (N_HEADS + KV_HEADS) * LANES), np.float32)
    for hd in range(N_HEADS):
        h, g = divmod(hd, GROUP)
        for p in range(3):
            sel[p, hd, hd * LANES + A_FQ + p] = 1.0
            sel[p, hd, (N_HEADS + h) * LANES + A_FK + 3 * g + p] = -1.0
            cst[0, hd * LANES + A_FK + 3 * g + p] = 1.0
    for h in range(KV_HEADS):
        for p in range(3):
            cst[0, (N_HEADS + h) * LANES + A_FQ + p] = 1.0
    return jnp.asarray(sel, BF), jnp.asarray(cst)


def _post_weights(b_forget, g_q, w_uq, g_kv, w_uk):
    uq = w_uq.reshape(MLA_Q_LORA, N_HEADS, MLA_QK)
    wn = uq[:, :, :MLA_NOPE].reshape(MLA_Q_LORA, N_HEADS * MLA_NOPE)
    wr = uq[:, :, MLA_NOPE:]
    wra = _pad_heads(wr.reshape(MLA_Q_LORA, N_HEADS * MLA_ROPE), N_HEADS, MLA_ROPE)
    wrb = _pad_heads(_rot_cols(wr).reshape(MLA_Q_LORA, N_HEADS * MLA_ROPE), N_HEADS, MLA_ROPE)
    eye = jnp.eye(N_HEADS, dtype=F32)
    wuk = jnp.einsum("lhn,hk->hnkl", w_uk, eye).reshape(N_HEADS * MLA_NOPE, N_HEADS * MLA_KV_LORA)
    sel, cst = _fox_select()
    bf = jnp.zeros((1, LANES), F32).at[0, M_FF:M_FF + N_HEADS].set(b_forget)
    return dict(bf=bf, gq=g_q.reshape(1, -1), gkv=g_kv.reshape(1, -1), wn=wn.astype(BF), wra=wra.astype(BF),
                wrb=wrb.astype(BF), wuk=wuk.astype(BF), sel=sel, cst=cst)


def _rope_table(pos):
    half = MLA_ROPE // 2
    inv = ROPE_THETA ** (-jnp.arange(half, dtype=F32) / half)
    ang = pos.astype(F32)[:, None] * inv[None, :]
    cos, sin = jnp.cos(ang), jnp.sin(ang)
    tab = jnp.concatenate([cos, cos, sin, sin], axis=1)
    return jnp.pad(tab, [(0, 0), (0, LANES - 2 * MLA_ROPE)])


def _tile(n, pref):
    return pref if n % pref == 0 else n


def kernel(x_prompt, x_sample, cache_fox_k, cache_fox_v, cache_fox_logf, cache_sb_k, cache_sb_v, cache_mla_ckv, cache_mla_kr, page_table, c_prompt, c_sample, w_ada, b_ada, g_mix, g_ffn, g_final, w_in, b_forget, g_mla_q, w_mla_uq, g_mla_kv, w_mla_uk, w_mla_uv, w_fox_o, w_sb_o, w_mla_o, w_o, w_router, b_router, w_exp_gate, w_exp_up, w_exp_down):
    bsz, seq, d = x_prompt.shape
    db, nt, _ = x_sample.shape
    depth = w_in.shape[0]
    n_pages = page_table.shape[1]
    page = cache_fox_k.shape[2]
    n_pool = cache_fox_k.shape[1]
    past = n_pages * page

    n_c = bsz + db
    c_all = jnp.pad(jnp.concatenate([c_prompt, c_sample], axis=0), [(0, (-n_c) % 8), (0, 0)])
    mod = _ada(c_all, w_ada, b_ada)

    w_in_r = [_in_weight(w_in[l]) for l in range(depth)]
    post_w = [_post_weights(b_forget[l], g_mla_q[l], w_mla_uq[l], g_mla_kv[l], w_mla_uk[l]) for l in range(depth)]
    wuv_pad = [jnp.pad(jnp.transpose(w_mla_uv[l], (1, 0, 2)), [(0, 0), (0, 0), (0, LANES - MLA_V)]).astype(BF)
               for l in range(depth)]
    wuv_cat = [w_mla_uv[l].reshape(MLA_KV_LORA, N_HEADS * MLA_V).astype(BF) for l in range(depth)]
    wf = [_pad_rows(w_fox_o[l], N_HEADS, HEAD_DIM).astype(BF) for l in range(depth)]
    ws = [_pad_rows(w_sb_o[l], N_HEADS, HEAD_DIM).astype(BF) for l in range(depth)]
    wm = [_pad_rows(w_mla_o[l], N_HEADS, MLA_V).astype(BF) for l in range(depth)]
    wo = w_o.astype(BF)
    wr_pad = jnp.pad(w_router, [(0, 0), (0, LANES - N_EXPERTS)])
    wr_hi = wr_pad.astype(BF)
    wr = jnp.stack([wr_hi, (wr_pad - wr_hi.astype(F32)).astype(BF)])
    br = jnp.pad(b_router, (0, LANES - N_EXPERTS)).reshape(1, LANES)
    wg, wu, wd = w_exp_gate.astype(BF), w_exp_up.astype(BF), w_exp_down.astype(BF)
    gfin = g_final.reshape(1, d)

    def tok_last(c):
        nd = c.ndim
        c = jnp.transpose(c, (0, 1) + tuple(range(3, nd)) + (2,))
        return c.reshape(c.shape[0], c.shape[1], -1, page)
    caches7 = (tok_last(cache_fox_k), tok_last(cache_fox_v), tok_last(cache_sb_k), tok_last(cache_sb_v),
               cache_mla_ckv, tok_last(cache_mla_kr), tok_last(cache_fox_logf))

    def mods(l, lo, n, rep):
        m = mod[l, lo:lo + n].reshape(n, 1, 6, d)
        if rep > 1:
            m = jnp.broadcast_to(m, (n, rep, 6, d)).reshape(1, n * rep, 6, d)
        return [m[:, :, k, :] for k in range(6)]

    def trunk(x, mod_lo, n_mod, rep, nb, s, pos, paged):
        t = nb * s
        tm = _tile(s, 512)
        tm_o = _tile(s, 256)
        tm_e = _tile(s, 1024)
        cs_tab = _rope_table(pos)
        rows = [[] for _ in range(7)]
        for l in range(depth):
            sh1, sc1, gt1, sh2, sc2, gt2 = mods(l, mod_lo, n_mod, rep)
            proj = _inproj(x, g_mix[l], sc1, sh1, w_in_r[l], tm)
            fq, sqq, kvb, mq, mk, lf, ckv, kr = _post(proj, cs_tab, post_w[l], nb, s, tm)
            kvf = proj[:, C_KV:C_KV + 4 * KV_HEADS * LANES].reshape(nb, s, 4, KV_HEADS, LANES)[..., :HEAD_DIM]
            for lst, r in zip(rows, (kvf[:, :, 0], kvf[:, :, 1], lf, kvf[:, :, 2], kvf[:, :, 3], ckv, kr)):
                lst.append(r)
            if not paged:
                tq = _tile(s, 256)
                o_f = _fox_attn(fq, kvb, tq, _tile(s, 512)).reshape(t, -1)
                o_s = _sb_attn(sqq, kvb, tq).reshape(t, -1)
                o_m = _mla_attn(mq, mk, wuv_pad[l], _tile(s, 128), _tile(s, 512)).reshape(t, -1)
            else:
                o_f, o_s, o_m = _decode(l, proj, fq, sqq, mq, kvf, lf, ckv, kr)
            x, h2 = _outproj(o_f, o_s, o_m, proj, x, gt1, sc2, sh2, g_ffn[l].reshape(1, d),
                             wf[l], ws[l], wm[l], wo[l], tm_o)
            x = _moe(h2, x, gt2, wr, br, wg[l], wu[l], wd[l], gfin, tm_e, l == depth - 1)
        return x, rows

    def _decode(l, proj, fq, sqq, mq, kvf, lf, ckv, kr):
        own = (jnp.arange(N_HEADS)[:, None] // GROUP == jnp.arange(KV_HEADS)[None, :])

        def q_bd(q):
            q = q.reshape(db, nt, N_HEADS, LANES)[..., :HEAD_DIM]
            q = jnp.where(own[None, None, :, :, None], q[:, :, :, None, :], jnp.zeros((), q.dtype))
            return q.reshape(db, nt * N_HEADS, KV_HEADS * HEAD_DIM)

        mq4 = mq.reshape(db, nt * N_HEADS, 2 * LANES)
        q4 = (q_bd(fq), q_bd(sqq), mq4[..., :LANES], mq4[..., LANES:LANES + MLA_ROPE])

        def new_t(r):
            r = r.reshape(db, nt, -1)
            return jnp.pad(jnp.transpose(r, (0, 2, 1)), [(0, 0), (0, 0), (0, page - nt)])

        ck_new = jnp.pad(ckv.reshape(db, nt, MLA_KV_LORA), [(0, 0), (0, page - nt), (0, 0)])
        kvr = kvf.reshape(db * nt, 4, KV_HEADS * HEAD_DIM)
        new7 = (new_t(kvr[:, 0]), new_t(kvr[:, 1]), new_t(kvr[:, 2]), new_t(kvr[:, 3]), ck_new,
                new_t(kr.reshape(db * nt, -1)), new_t(lf.reshape(db * nt, -1)))
        o_f, o_s, o_m = _decode_attn(page_table, q4, new7, caches7, wuv_cat[l], l, nt)

        def own_block(o, width):
            o = o.reshape(db, nt, N_HEADS, -1, width)
            idx = (jnp.arange(N_HEADS) // (N_HEADS // o.shape[3]))
            o = jnp.take_along_axis(o, idx[None, None, :, None, None], axis=3)[:, :, :, 0]
            o = jnp.pad(o, [(0, 0), (0, 0), (0, 0), (0, LANES - width)])
            return o.reshape(db * nt, N_HEADS * LANES).astype(BF)

        return own_block(o_f, HEAD_DIM), own_block(o_s, HEAD_DIM), own_block(o_m, MLA_V)

    y_p, rows_p = trunk(x_prompt.reshape(bsz * seq, d), 0, bsz, 1, bsz, seq, jnp.arange(seq), False)
    pos_s = jnp.tile(past + jnp.arange(nt), db)
    y_s, rows_s = trunk(x_sample.reshape(db * nt, d), bsz, db, nt, 1, db * nt, pos_s, True)

    def stack(rows, nb, s):
        shapes = [(KV_HEADS, HEAD_DIM), (KV_HEADS, HEAD_DIM), (N_HEADS,), (KV_HEADS, HEAD_DIM),
                  (KV_HEADS, HEAD_DIM), (MLA_KV_LORA,), (MLA_ROPE,)]
        return [jnp.stack([r.reshape((nb, s) + sh) for r in lst]) for lst, sh in zip(rows, shapes)]

    return (y_p.reshape(bsz, seq, d), y_s.reshape(db, nt, d), *stack(rows_p, bsz, seq), *stack(rows_s, db, nt))
```

```python
import functools

import jax
import jax.numpy as jnp
import numpy as np
from jax import lax
from jax.experimental import pallas as pl
from jax.experimental.pallas import tpu as pltpu

F32 = jnp.float32
BF = jnp.bfloat16

HEAD_DIM = 64
N_HEADS = 8
KV_HEADS = 2
GROUP = N_HEADS // KV_HEADS
MLA_NOPE = 64
MLA_ROPE = 32
MLA_QK = MLA_NOPE + MLA_ROPE
MLA_V = 64
MLA_Q_LORA = 256
MLA_KV_LORA = 128
ROPE_THETA = 10000.0
N_EXPERTS = 16
EXPERTS_PER_GROUP = 4
NORM_EPS = 1e-6
LANES = 128
NEG = -1e30
EXP_UNDERFLOW = -104.0

C_FQ, C_SQ, C_KV, C_GATE, C_MLA = 0, 1024, 2048, 3072, 6144
IN_W = 6656
M_KR, M_KRROT, M_FF = 0, 32, 64
A_FQ = 64
A_FK = 67

VMEM_LIMIT = 56 * 1024 * 1024


def _cparams(sem):
    return pltpu.CompilerParams(dimension_semantics=sem, vmem_limit_bytes=VMEM_LIMIT)


def _split3(x):
    hi = x.astype(BF)
    r1 = x - hi.astype(F32)
    mid = r1.astype(BF)
    lo = (r1 - mid.astype(F32)).astype(BF)
    return hi, mid, lo


def _softplus(z):
    return jnp.maximum(z, 0.0) + jnp.log1p(jnp.exp(-jnp.abs(z)))


def _ada_kernel(c_ref, w_ref, b_ref, o_ref):
    c = c_ref[...]
    a = (c * jax.nn.sigmoid(c)).astype(BF)
    o_ref[...] = jnp.dot(a, w_ref[...].astype(BF), preferred_element_type=F32) + b_ref[...]


def _ada(c_all, w_ada, b_ada):
    depth, d, n = w_ada.shape
    r = c_all.shape[0]
    tn = 1536
    return pl.pallas_call(
        _ada_kernel,
        grid=(depth, n // tn),
        in_specs=[pl.BlockSpec((r, d), lambda l, j: (0, 0)),
                  pl.BlockSpec((None, d, tn), lambda l, j: (l, 0, j)),
                  pl.BlockSpec((None, 1, tn), lambda l, j: (l, 0, j))],
        out_specs=pl.BlockSpec((None, r, tn), lambda l, j: (l, 0, j)),
        out_shape=jax.ShapeDtypeStruct((depth, r, n), F32),
        compiler_params=_cparams(("arbitrary", "arbitrary")),
        name="ada_mod",
    )(c_all, w_ada, b_ada.reshape(depth, 1, n))


def _inproj_kernel(x_ref, g_ref, sc_ref, sh_ref, w_ref, o_ref, h_ref):
    @pl.when(pl.program_id(1) == 0)
    def _():
        x = x_ref[...]
        y = x * lax.rsqrt(jnp.mean(x * x, axis=-1, keepdims=True) + NORM_EPS) * g_ref[...]
        h_ref[...] = (y * (1.0 + sc_ref[...]) + sh_ref[...]).astype(BF)

    o_ref[...] = jnp.dot(h_ref[...], w_ref[...], preferred_element_type=F32)


def _mod_spec(m, t, tm):
    nb, r, d = m.shape
    if r == 1:
        per = (t // tm) // nb
        return pl.BlockSpec((None, 1, d), lambda i, *_: (i // per, 0, 0))
    return pl.BlockSpec((None, tm, d), lambda i, *_: (0, i, 0))


def _inproj(x, g, sc, sh, w, tm):
    t, d = x.shape
    n = w.shape[1]
    tn = 512
    return pl.pallas_call(
        _inproj_kernel,
        grid=(t // tm, n // tn),
        in_specs=[pl.BlockSpec((tm, d), lambda i, j: (i, 0)),
                  pl.BlockSpec((1, d), lambda i, j: (0, 0)),
                  _mod_spec(sc, t, tm), _mod_spec(sh, t, tm),
                  pl.BlockSpec((d, tn), lambda i, j: (0, j))],
        out_specs=pl.BlockSpec((tm, tn), lambda i, j: (i, j)),
        out_shape=jax.ShapeDtypeStruct((t, n), F32),
        scratch_shapes=[pltpu.VMEM((tm, d), BF)],
        compiler_params=_cparams(("arbitrary", "arbitrary")),
        name="in_proj",
    )(x, g.reshape(1, d), sc, sh, w)


def _post_kernel(qf_ref, qs_ref, kv_ref, ml_ref, cs_ref, bf_ref, gq_ref, gkv_ref,
                 wn_ref, wra_ref, wrb_ref, wuk_ref, sel_ref, cst_ref,
                 fq_o, sq_o, kvb_o, mq_o, mk_o, lf_o, ckv_o, kr_o, carry_ref, *, mla_scale):
    tm = qf_ref.shape[0]

    @pl.when(pl.program_id(1) == 0)
    def _():
        carry_ref[...] = jnp.zeros_like(carry_ref)

    lane = lax.broadcasted_iota(jnp.int32, (tm, LANES), 1)
    misc = ml_ref[:, MLA_Q_LORA + MLA_KV_LORA:]
    cs = cs_ref[...]

    t1 = misc * cs
    kr = jnp.where(lane < MLA_ROPE, t1 + pltpu.roll(t1, LANES - M_KRROT, 1), 0.0)
    kr_o[...] = kr[:, :MLA_ROPE]

    lf = pltpu.roll(misc + bf_ref[...], LANES - M_FF, 1)
    lf = jnp.where(lane < N_HEADS, -_softplus(-lf), 0.0)
    lf_o[...] = lf[:, :N_HEADS]

    row = lax.broadcasted_iota(jnp.int32, (tm, tm), 0)
    col = lax.broadcasted_iota(jnp.int32, (tm, tm), 1)
    tri = (col <= row).astype(BF)
    hi, mid, lo = _split3(lf)
    f_cum = (jnp.dot(tri, hi, preferred_element_type=F32)
             + jnp.dot(tri, mid, preferred_element_type=F32)
             + jnp.dot(tri, lo, preferred_element_type=F32)) + carry_ref[...]
    carry_ref[...] = f_cum[tm - 1:tm, :]
    fh, fm, fl = _split3(f_cum)
    ext = (jnp.dot(fh, sel_ref[0], preferred_element_type=F32)
           + jnp.dot(fm, sel_ref[1], preferred_element_type=F32)
           + jnp.dot(fl, sel_ref[2], preferred_element_type=F32)) + cst_ref[...]

    scale = HEAD_DIM ** -0.5
    fq_o[...] = (qf_ref[...] * scale + ext[:, :N_HEADS * LANES]).astype(BF)
    sq_o[...] = (qs_ref[...] * scale).astype(BF)
    kv = kv_ref[...]
    kvb_o[:, :KV_HEADS * LANES] = (kv[:, :KV_HEADS * LANES] + ext[:, N_HEADS * LANES:]).astype(BF)
    kvb_o[:, KV_HEADS * LANES:] = kv[:, KV_HEADS * LANES:].astype(BF)

    mdq = ml_ref[:, :MLA_Q_LORA]
    cq = (mdq * lax.rsqrt(jnp.mean(mdq * mdq, axis=-1, keepdims=True) + NORM_EPS) * gq_ref[...]).astype(BF)
    nope = jnp.dot(cq, wn_ref[...], preferred_element_type=F32).astype(BF)
    q_lat = jnp.dot(nope, wuk_ref[...], preferred_element_type=F32)
    ra = jnp.dot(cq, wra_ref[...], preferred_element_type=F32)
    rb = jnp.dot(cq, wrb_ref[...], preferred_element_type=F32)
    cos_t = jnp.where(lane < MLA_ROPE, cs, 0.0)
    sin_t = jnp.where(lane < MLA_ROPE, pltpu.roll(cs, LANES - MLA_ROPE, 1), 0.0)
    for h in range(N_HEADS):
        sl = slice(h * LANES, (h + 1) * LANES)
        mq_o[:, 2 * h * LANES:(2 * h + 1) * LANES] = (q_lat[:, sl] * mla_scale).astype(BF)
        mq_o[:, (2 * h + 1) * LANES:(2 * h + 2) * LANES] = (
            (ra[:, sl] * cos_t + rb[:, sl] * sin_t) * mla_scale).astype(BF)

    craw = ml_ref[:, MLA_Q_LORA:MLA_Q_LORA + MLA_KV_LORA]
    ckv = craw * lax.rsqrt(jnp.mean(craw * craw, axis=-1, keepdims=True) + NORM_EPS) * gkv_ref[...]
    ckv_o[...] = ckv
    mk_o[:, :LANES] = ckv.astype(BF)
    mk_o[:, LANES:] = kr.astype(BF)


def _post(proj, cs_tab, pw, nb, s, tm):
    p3 = proj.reshape(nb, s, IN_W)
    full = lambda a: pl.BlockSpec(a.shape, lambda b, i: (0,) * a.ndim)
    seg = lambda w, c: pl.BlockSpec((None, tm, w), lambda b, i: (b, i, c // w))
    row = lambda w: pl.BlockSpec((None, tm, w), lambda b, i: (b, i, 0))
    outs = [(N_HEADS * LANES, BF), (N_HEADS * LANES, BF), (4 * KV_HEADS * LANES, BF),
            (N_HEADS * 2 * LANES, BF), (2 * LANES, BF), (N_HEADS, F32), (MLA_KV_LORA, F32), (MLA_ROPE, F32)]
    return pl.pallas_call(
        functools.partial(_post_kernel, mla_scale=MLA_QK ** -0.5),
        grid=(nb, s // tm),
        in_specs=[seg(1024, C_FQ), seg(1024, C_SQ), seg(1024, C_KV), seg(512, C_MLA),
                  pl.BlockSpec((tm, LANES), lambda b, i: (i, 0)),
                  full(pw["bf"]), full(pw["gq"]), full(pw["gkv"]), full(pw["wn"]), full(pw["wra"]),
                  full(pw["wrb"]), full(pw["wuk"]), full(pw["sel"]), full(pw["cst"])],
        out_specs=[row(w) for w, _ in outs],
        out_shape=[jax.ShapeDtypeStruct((nb, s, w), dt) for w, dt in outs],
        scratch_shapes=[pltpu.VMEM((1, LANES), F32)],
        compiler_params=_cparams(("arbitrary", "arbitrary")),
        name="post_proj",
    )(p3, p3, p3, p3, cs_tab, pw["bf"], pw["gq"], pw["gkv"], pw["wn"], pw["wra"], pw["wrb"],
      pw["wuk"], pw["sel"], pw["cst"])


def _stack_heads(q_ref, n, w):
    return jnp.concatenate([q_ref[:, g * w:(g + 1) * w] for g in range(n)], axis=0)


def _softmax_block(q, k, v, m_ref, l_ref, acc_ref, mask, skip_dead=False):
    s = lax.dot_general(q, k, (((1,), (1,)), ((), ())), preferred_element_type=F32)
    if mask is not None:
        s = jnp.where(mask, s, NEG)
    m_prev = m_ref[...]
    m_cur = jnp.max(s, axis=1)[:, None]

    def update():
        m_next = jnp.maximum(m_prev, m_cur)
        p = jnp.exp(s - jnp.tile(m_next, (1, s.shape[1] // LANES)))
        alpha = jnp.exp(m_prev - m_next)
        l_ref[...] = alpha * l_ref[...] + jnp.sum(p, axis=1)[:, None]
        m_ref[...] = m_next
        acc_ref[...] = acc_ref[...] * alpha + jnp.dot(p.astype(BF), v, preferred_element_type=F32)

    if skip_dead:
        pl.when(jnp.max(m_cur - m_prev) > EXP_UNDERFLOW)(update)
    else:
        update()


def _causal_mask(i, tq, j0, tk, n_rep, strict):
    qpos = i * tq + lax.broadcasted_iota(jnp.int32, (tq, tk), 0)
    kpos = j0 + lax.broadcasted_iota(jnp.int32, (tq, tk), 1)
    m = (kpos < qpos) if strict else (kpos <= qpos)
    return jnp.tile(m, (n_rep, 1))


def _fox_kernel(q_ref, k_ref, v_ref, o_ref, m_ref, l_ref, acc_ref, *, tq, tk):
    i = pl.program_id(2)
    q = _stack_heads(q_ref, GROUP, LANES)
    m_ref[...] = jnp.full_like(m_ref, NEG)
    l_ref[...] = jnp.zeros_like(l_ref)
    acc_ref[...] = jnp.zeros_like(acc_ref)
    n_full = (i * tq) // tk

    off = pl.multiple_of(n_full * tk, tk)
    _softmax_block(q, k_ref[pl.ds(off, tk), :], v_ref[pl.ds(off, tk), :], m_ref, l_ref, acc_ref,
                   _causal_mask(i, tq, off, tk, GROUP, False))

    def body(n, c):
        o2 = pl.multiple_of((n_full - 1 - n) * tk, tk)
        _softmax_block(q, k_ref[pl.ds(o2, tk), :], v_ref[pl.ds(o2, tk), :], m_ref, l_ref, acc_ref, None,
                       skip_dead=True)
        return c

    lax.fori_loop(0, n_full, body, 0)
    out = acc_ref[...] / l_ref[...]
    for g in range(GROUP):
        o_ref[:, g * LANES:(g + 1) * LANES] = out[g * tq:(g + 1) * tq].astype(o_ref.dtype)


def _fox_attn(fq, kvb, tq, tk):
    b, s, _ = fq.shape
    m = GROUP * tq
    return pl.pallas_call(
        functools.partial(_fox_kernel, tq=tq, tk=tk),
        grid=(b, KV_HEADS, s // tq),
        in_specs=[pl.BlockSpec((None, tq, GROUP * LANES), lambda bb, h, i: (bb, i, h)),
                  pl.BlockSpec((None, s, LANES), lambda bb, h, i: (bb, 0, h)),
                  pl.BlockSpec((None, s, LANES), lambda bb, h, i: (bb, 0, KV_HEADS + h))],
        out_specs=pl.BlockSpec((None, tq, GROUP * LANES), lambda bb, h, i: (bb, i, h)),
        out_shape=jax.ShapeDtypeStruct((b, s, N_HEADS * LANES), BF),
        scratch_shapes=[pltpu.VMEM((m, LANES), F32), pltpu.VMEM((m, LANES), F32), pltpu.VMEM((m, LANES), F32)],
        compiler_params=_cparams(("arbitrary", "arbitrary", "arbitrary")),
        name="fox_attn",
    )(fq, kvb, kvb)


def _suffix_in_block(x, upper, pieces=3):
    parts = _split3(x)[:pieces]
    r = x.shape[0]
    res = jnp.dot(jnp.concatenate(parts, axis=0), upper, preferred_element_type=F32)
    out = res[:r]
    for p in range(1, pieces):
        out = out + res[p * r:(p + 1) * r]
    return out


def _sb_block(q, k, v, upper, r_ref, acc_ref, mask):
    z = lax.dot_general(q, k, (((1,), (1,)), ((), ())), preferred_element_type=F32)
    sp = _softplus(z)
    l1m = -sp if mask is None else jnp.where(mask, -sp, 0.0)
    later = _suffix_in_block(l1m, upper, pieces=2)
    e = z - sp + later + jnp.tile(r_ref[...], (1, z.shape[1] // LANES))
    a = jnp.exp(e) if mask is None else jnp.where(mask, jnp.exp(e), 0.0)
    acc_ref[...] += jnp.dot(a.astype(BF), v, preferred_element_type=F32)
    r_ref[...] += jnp.sum(l1m, axis=1)[:, None]


def _sb_kernel(q_ref, k_ref, v_ref, u_ref, o_ref, r_ref, acc_ref, *, tq):
    i = pl.program_id(2)
    q = _stack_heads(q_ref, GROUP, LANES)
    r_ref[...] = jnp.zeros_like(r_ref)
    acc_ref[...] = jnp.zeros_like(acc_ref)
    upper = u_ref[...]
    off = pl.multiple_of(i * tq, tq)
    _sb_block(q, k_ref[pl.ds(off, tq), :], v_ref[pl.ds(off, tq), :], upper, r_ref, acc_ref,
              _causal_mask(i, tq, off, tq, GROUP, True))

    def live():
        return jnp.max(r_ref[...]) > EXP_UNDERFLOW

    def body(c):
        n, _ = c
        o2 = pl.multiple_of((i - 1 - n) * tq, tq)
        _sb_block(q, k_ref[pl.ds(o2, tq), :], v_ref[pl.ds(o2, tq), :], upper, r_ref, acc_ref, None)
        return n + 1, live()

    lax.while_loop(lambda c: jnp.logical_and(c[0] < i, c[1]), body, (jnp.int32(0), live()))
    out = acc_ref[...]
    for g in range(GROUP):
        o_ref[:, g * LANES:(g + 1) * LANES] = out[g * tq:(g + 1) * tq].astype(o_ref.dtype)


def _upper(w):
    j = lax.broadcasted_iota(jnp.int32, (w, w), 0)
    s = lax.broadcasted_iota(jnp.int32, (w, w), 1)
    return (j > s).astype(BF)


def _sb_attn(sq, kvb, tq):
    b, s, _ = sq.shape
    m = GROUP * tq
    return pl.pallas_call(
        functools.partial(_sb_kernel, tq=tq),
        grid=(b, KV_HEADS, s // tq),
        in_specs=[pl.BlockSpec((None, tq, GROUP * LANES), lambda bb, h, i: (bb, i, h)),
                  pl.BlockSpec((None, s, LANES), lambda bb, h, i: (bb, 0, 2 * KV_HEADS + h)),
                  pl.BlockSpec((None, s, LANES), lambda bb, h, i: (bb, 0, 3 * KV_HEADS + h)),
                  pl.BlockSpec((tq, tq), lambda bb, h, i: (0, 0))],
        out_specs=pl.BlockSpec((None, tq, GROUP * LANES), lambda bb, h, i: (bb, i, h)),
        out_shape=jax.ShapeDtypeStruct((b, s, N_HEADS * LANES), BF),
        scratch_shapes=[pltpu.VMEM((m, LANES), F32), pltpu.VMEM((m, LANES), F32)],
        compiler_params=_cparams(("arbitrary", "arbitrary", "arbitrary")),
        name="sb_attn",
    )(sq, kvb, kvb, _upper(tq))


def _mla_kernel(q_ref, k_ref, wuv_ref, o_ref, m_ref, l_ref, acc_ref, *, tq, tk):
    i = pl.program_id(1)
    q = _stack_heads(q_ref, N_HEADS, 2 * LANES)
    m_ref[...] = jnp.full_like(m_ref, NEG)
    l_ref[...] = jnp.zeros_like(l_ref)
    acc_ref[...] = jnp.zeros_like(acc_ref)
    n_full = (i * tq) // tk

    def body(j, c):
        off = pl.multiple_of(j * tk, tk)
        kk = k_ref[pl.ds(off, tk), :]
        _softmax_block(q, kk, kk[:, :LANES], m_ref, l_ref, acc_ref, None)
        return c

    lax.fori_loop(0, n_full, body, 0)
    off = pl.multiple_of(n_full * tk, tk)
    kk = k_ref[pl.ds(off, tk), :]
    _softmax_block(q, kk, kk[:, :LANES], m_ref, l_ref, acc_ref, _causal_mask(i, tq, off, tk, N_HEADS, False))
    o_lat = (acc_ref[...] / l_ref[...]).astype(BF)
    for h in range(N_HEADS):
        o_ref[:, h * LANES:(h + 1) * LANES] = jnp.dot(
            o_lat[h * tq:(h + 1) * tq], wuv_ref[h], preferred_element_type=F32).astype(o_ref.dtype)


def _mla_attn(mq, mk, wuv, tq, tk):
    b, s, _ = mq.shape
    m = N_HEADS * tq
    return pl.pallas_call(
        functools.partial(_mla_kernel, tq=tq, tk=tk),
        grid=(b, s // tq),
        in_specs=[pl.BlockSpec((None, tq, N_HEADS * 2 * LANES), lambda bb, i: (bb, i, 0)),
                  pl.BlockSpec((None, s, 2 * LANES), lambda bb, i: (bb, 0, 0)),
                  pl.BlockSpec(wuv.shape, lambda bb, i: (0, 0, 0))],
        out_specs=pl.BlockSpec((None, tq, N_HEADS * LANES), lambda bb, i: (bb, i, 0)),
        out_shape=jax.ShapeDtypeStruct((b, s, N_HEADS * LANES), BF),
        scratch_shapes=[pltpu.VMEM((m, LANES), F32), pltpu.VMEM((m, LANES), F32), pltpu.VMEM((m, LANES), F32)],
        compiler_params=_cparams(("arbitrary", "arbitrary")),
        name="mla_attn",
    )(mq, mk, wuv)


def _suffix_lanes(x, upper, bw):
    w = x.shape[1]
    carry = jnp.zeros((x.shape[0], 1), F32)
    outs = [None] * (w // bw)
    for b in reversed(range(w // bw)):
        xb = x[:, b * bw:(b + 1) * bw]
        outs[b] = _suffix_in_block(xb, upper) + carry
        carry = carry + jnp.sum(xb, axis=1, keepdims=True)
    return (outs[0] if len(outs) == 1 else jnp.concatenate(outs, axis=1)), carry


def _online(s, v_nt, v_n, m_ref, l_ref, acc_ref):
    m_prev = m_ref[...]
    m_next = jnp.maximum(m_prev, jnp.max(s, axis=1, keepdims=True))
    p = jnp.exp(s - m_next)
    alpha = jnp.exp(m_prev - m_next)
    l_ref[...] = alpha * l_ref[...] + jnp.sum(p, axis=1, keepdims=True)
    m_ref[...] = m_next
    pb = p.astype(BF)
    if v_nt is not None:
        pv = lax.dot_general(pb, v_nt, (((1,), (1,)), ((), ())), preferred_element_type=F32)
    else:
        pv = jnp.dot(pb, v_n, preferred_element_type=F32)
    acc_ref[...] = acc_ref[...] * alpha + pv


def _decode_block(qf, qs, ql, qr, fk, fv, sk, sv, ck, kr, lf, upper, bw, st, new_mask, cn):
    (fm, fl, fa, sr, sa, mm, mlr, ma, rf) = st
    nt = qf.shape[0] // N_HEADS
    w = fk.shape[1]
    s = jnp.dot(qf, fk.astype(BF), preferred_element_type=F32)
    if new_mask is None:
        later, tot = _suffix_lanes(lf, upper, bw)
        bias = later + rf[...]
        s = jnp.concatenate([s[t * N_HEADS:(t + 1) * N_HEADS] + (bias + cn[:, t:t + 1]) for t in range(nt)], axis=0)
        rf[...] += tot
    else:
        le, strict = new_mask
        s = jnp.concatenate([s[t * N_HEADS:(t + 1) * N_HEADS] + (cn[:, t:t + 1] - cn) for t in range(nt)], axis=0)
        s = jnp.where(le, s, NEG)
    _online(s, fv.astype(BF), None, fm, fl, fa)
    z = jnp.dot(qs, sk.astype(BF), preferred_element_type=F32)
    sp = _softplus(z)
    l1m = -sp if new_mask is None else jnp.where(new_mask[1], -sp, 0.0)
    later, tot = _suffix_lanes(l1m, upper, bw)
    e = jnp.exp(z - sp + later + sr[...])
    a = e if new_mask is None else jnp.where(new_mask[1], e, 0.0)
    sa[...] += lax.dot_general(a.astype(BF), sv.astype(BF), (((1,), (1,)), ((), ())), preferred_element_type=F32)
    sr[...] += tot
    ckb = ck.astype(BF)
    s = (lax.dot_general(ql, ckb, (((1,), (1,)), ((), ())), preferred_element_type=F32)
         + jnp.dot(qr, kr.astype(BF), preferred_element_type=F32))
    if new_mask is not None:
        s = jnp.where(new_mask[0], s, NEG)
    _online(s, None, ckb, mm, mlr, ma)


def _decode_kernel(pt_ref, qf_ref, qs_ref, ql_ref, qr_ref, fkn_ref, fvn_ref, skn_ref, svn_ref, ckn_ref,
                   krn_ref, lfn_ref, u_ref, wuv_ref,
                   fk_hbm, fv_hbm, sk_hbm, sv_hbm, ck_hbm, kr_hbm, lf_hbm,
                   of_ref, os_ref, om_ref,
                   fk_b, fv_b, sk_b, sv_b, ck_b, kr_b, lf_b, sem,
                   fm, fl, fa, sr, sa, mm, mlr, ma, rf, cn_ref, *, layer, ppc, nt):
    sq = pl.program_id(0)
    c = pl.program_id(1)
    n_seq = pl.num_programs(0)
    page = fk_b.shape[1]
    hbm = (fk_hbm, fv_hbm, sk_hbm, sv_hbm, ck_hbm, kr_hbm, lf_hbm)
    bufs = (fk_b, fv_b, sk_b, sv_b, ck_b, kr_b, lf_b)

    def copies(seq, first_page, slot):
        out = []
        for p in range(ppc):
            pg = pt_ref[seq, first_page + p]
            for a in range(7):
                if a == 4:
                    dst = bufs[a].at[slot, pl.ds(p * page, page), :]
                else:
                    dst = bufs[a].at[slot, :, pl.ds(p * page, page)]
                out.append(pltpu.make_async_copy(hbm[a].at[layer, pg], dst, sem.at[slot, a]))
        return out

    def start(seq, first_page, slot):
        for cp in copies(seq, first_page, slot):
            cp.start()

    def wait(slot):
        for cp in copies(0, 0, slot):
            cp.wait()

    @pl.when(jnp.logical_and(sq == 0, c == 0))
    def _():
        start(0, ppc, 0)

    qf, qs, ql, qr = qf_ref[...], qs_ref[...], ql_ref[...], qr_ref[...]
    upper = u_ref[...]
    st = (fm, fl, fa, sr, sa, mm, mlr, ma, rf)
    rows = nt * N_HEADS

    @pl.when(c == 0)
    def _():
        start(sq, 0, 1)
        for r in (fm, mm):
            r[...] = jnp.full_like(r, NEG)
        for r in (fl, fa, sr, sa, mlr, ma, rf):
            r[...] = jnp.zeros_like(r)
        lfn = lfn_ref[...]
        lane8 = lax.broadcasted_iota(jnp.int32, (N_HEADS, LANES), 1)
        later, tot = _suffix_lanes(jnp.where(lane8 < nt, lfn, 0.0), upper[:LANES, :LANES], LANES)
        cn = tot - later
        cn_ref[...] = cn
        trow = lax.broadcasted_iota(jnp.int32, (rows, LANES), 0) // N_HEADS
        kcol = lax.broadcasted_iota(jnp.int32, (rows, LANES), 1)
        _decode_block(qf, qs, ql, qr, fkn_ref[...], fvn_ref[...], skn_ref[...], svn_ref[...], ckn_ref[...],
                      krn_ref[...], None, upper[:LANES, :LANES], LANES, st, (kcol <= trow, kcol < trow), cn)
        wait(0)
        _decode_block(qf, qs, ql, qr, fk_b[0], fv_b[0], sk_b[0], sv_b[0], ck_b[0], kr_b[0], lf_b[0],
                      upper, upper.shape[0], st, None, cn)

    @pl.when(c == 1)
    def _():
        @pl.when(sq + 1 < n_seq)
        def _():
            start(sq + 1, ppc, 0)
        wait(1)
        _decode_block(qf, qs, ql, qr, fk_b[1], fv_b[1], sk_b[1], sv_b[1], ck_b[1], kr_b[1], lf_b[1],
                      upper, upper.shape[0], st, None, cn_ref[...])
        of_ref[...] = fa[...] / fl[...]
        os_ref[...] = sa[...]
        o_lat = (ma[...] / mlr[...]).astype(BF)
        om_ref[...] = jnp.dot(o_lat, wuv_ref[...], preferred_element_type=F32)


def _decode_attn(page_table, q4, new7, caches7, wuv_cat, layer, nt):
    qf, qs, ql, qr = q4
    db, rows, _ = qf.shape
    n_pages = page_table.shape[1]
    ppc = n_pages // 2
    page = caches7[0].shape[-1]
    w = ppc * page
    bw = min(256, w)
    upper = _upper(bw)
    per_seq = lambda a: pl.BlockSpec((None,) + a.shape[1:], lambda s, c, pt: (s,) + (0,) * (a.ndim - 1))
    full = lambda a: pl.BlockSpec(a.shape, lambda s, c, pt: (0,) * a.ndim)
    anyspec = pl.BlockSpec(memory_space=pl.ANY)
    out_w = (LANES, LANES, wuv_cat.shape[1])
    small = lambda wd: pltpu.VMEM((rows, wd), F32)
    grid_spec = pltpu.PrefetchScalarGridSpec(
        num_scalar_prefetch=1,
        grid=(db, 2),
        in_specs=[per_seq(a) for a in (qf, qs, ql, qr) + tuple(new7)] + [full(upper), full(wuv_cat)] + [anyspec] * 7,
        out_specs=[pl.BlockSpec((None, rows, wd), lambda s, c, pt: (s, 0, 0)) for wd in out_w],
        scratch_shapes=[pltpu.VMEM((2, LANES, w), F32)] * 4
        + [pltpu.VMEM((2, w, LANES), F32), pltpu.VMEM((2, MLA_ROPE, w), F32), pltpu.VMEM((2, N_HEADS, w), F32),
           pltpu.SemaphoreType.DMA((2, 7)),
           small(1), small(1), small(LANES), small(1), small(LANES), small(1), small(1), small(LANES),
           pltpu.VMEM((N_HEADS, 1), F32), pltpu.VMEM((N_HEADS, LANES), F32)],
    )
    return pl.pallas_call(
        functools.partial(_decode_kernel, layer=layer, ppc=ppc, nt=nt),
        grid_spec=grid_spec,
        out_shape=[jax.ShapeDtypeStruct((db, rows, wd), F32) for wd in out_w],
        compiler_params=_cparams(("arbitrary", "arbitrary")),
        name="decode_attn",
    )(page_table, qf, qs, ql, qr, *new7, upper, wuv_cat, *caches7)


def _outproj_kernel(of_ref, os_ref, om_ref, ga_ref, gb_ref, gc_ref, x_ref, gt_ref, sc_ref, sh_ref, g_ref,
                    wf_ref, ws_ref, wm_ref, wo_ref, wr_ref, xo_ref, ho_ref, lo_ref):
    yf = jnp.dot(of_ref[...], wf_ref[...], preferred_element_type=F32)
    ys = jnp.dot(os_ref[...], ws_ref[...], preferred_element_type=F32)
    ym = jnp.dot(om_ref[...], wm_ref[...], preferred_element_type=F32)
    merged = (jax.nn.sigmoid(ga_ref[...]) * yf + jax.nn.sigmoid(gb_ref[...]) * ys
              + jax.nn.sigmoid(gc_ref[...]) * ym)
    mix = jnp.dot(merged.astype(BF), wo_ref[...], preferred_element_type=F32)
    x = x_ref[...] + gt_ref[...] * mix
    xo_ref[...] = x
    y = x * lax.rsqrt(jnp.mean(x * x, axis=-1, keepdims=True) + NORM_EPS) * g_ref[...]
    h = y * (1.0 + sc_ref[...]) + sh_ref[...]
    hh = h.astype(BF)
    ho_ref[...] = hh
    hl = (h - hh.astype(F32)).astype(BF)
    lo_ref[...] = (jnp.dot(hh, wr_ref[0], preferred_element_type=F32)
                   + jnp.dot(hh, wr_ref[1], preferred_element_type=F32)
                   + jnp.dot(hl, wr_ref[0], preferred_element_type=F32))


def _outproj(o_f, o_s, o_m, proj, x, gt, sc, sh, g, wf, ws, wm, wo, wr, tm):
    t, d = x.shape
    tok = lambda w: pl.BlockSpec((tm, w), lambda i: (i, 0))
    gate = lambda k: pl.BlockSpec((tm, d), lambda i: (i, C_GATE // d + k))
    mod = _mod_spec(gt, t, tm)
    full = lambda a: pl.BlockSpec(a.shape, lambda i: (0,) * a.ndim)
    return pl.pallas_call(
        _outproj_kernel,
        grid=(t // tm,),
        in_specs=[tok(o_f.shape[1]), tok(o_s.shape[1]), tok(o_m.shape[1]), gate(0), gate(1), gate(2), tok(d),
                  mod, mod, mod, full(g), full(wf), full(ws), full(wm), full(wo), full(wr)],
        out_specs=[tok(d), tok(d), tok(LANES)],
        out_shape=[jax.ShapeDtypeStruct((t, d), F32), jax.ShapeDtypeStruct((t, d), BF),
                   jax.ShapeDtypeStruct((t, LANES), F32)],
        compiler_params=_cparams(("arbitrary",)),
        name="out_proj",
    )(o_f, o_s, o_m, proj, proj, proj, x, gt, sc, sh, g, wf, ws, wm, wo, wr)


def _route(logits, bias):
    tm = logits.shape[0]
    lane = lax.broadcasted_iota(jnp.int32, (tm, LANES), 1)
    scores = jax.nn.sigmoid(logits)
    sel = scores + bias
    pos = lane % EXPERTS_PER_GROUP
    grp = lane // EXPERTS_PER_GROUP
    n_groups = N_EXPERTS // EXPERTS_PER_GROUP

    def shifted(x, k):
        return pltpu.roll(x, (LANES - k) % LANES, 1)

    def beats(other, mine, k):
        return (other > mine) | ((other == mine) & (k < 0))

    rank = jnp.zeros((tm, LANES), jnp.int32)
    for k in range(-EXPERTS_PER_GROUP + 1, EXPERTS_PER_GROUP):
        if k == 0:
            continue
        same = (pos + k >= 0) & (pos + k < EXPERTS_PER_GROUP)
        rank += (same & beats(shifted(sel, k), sel, k)).astype(jnp.int32)
    top2 = jnp.where(rank < 2, sel, 0.0)
    gscore = top2
    for k in range(-EXPERTS_PER_GROUP + 1, EXPERTS_PER_GROUP):
        if k == 0:
            continue
        same = (pos + k >= 0) & (pos + k < EXPERTS_PER_GROUP)
        gscore += jnp.where(same, shifted(top2, k), 0.0)
    grank = jnp.zeros((tm, LANES), jnp.int32)
    for k in range(-n_groups + 1, n_groups):
        if k == 0:
            continue
        ok = (grp + k >= 0) & (grp + k < n_groups)
        grank += (ok & beats(shifted(gscore, k * EXPERTS_PER_GROUP), gscore, k)).astype(jnp.int32)
    chosen = (grank == 0) & (rank < 2) & (lane < N_EXPERTS)
    wts = jnp.where(chosen, scores, 0.0)
    return wts / jnp.sum(wts, axis=1, keepdims=True)


def _moe_kernel(h_ref, x_ref, gt_ref, lg_ref, br_ref, wg_ref, wu_ref, wd_ref, gf_ref, o_ref,
                comb_ref, acc_ref, *, final):
    e = pl.program_id(1)
    h = h_ref[...]

    @pl.when(e == 0)
    def _():
        comb_ref[...] = _route(lg_ref[...], br_ref[...])
        acc_ref[...] = jnp.zeros_like(acc_ref)

    a = jnp.dot(h, wg_ref[...], preferred_element_type=F32)
    u = jnp.dot(h, wu_ref[...], preferred_element_type=F32)
    hid = (a * jax.nn.sigmoid(a) * u).astype(BF)
    y = jnp.dot(hid, wd_ref[...], preferred_element_type=F32)
    lane = lax.broadcasted_iota(jnp.int32, comb_ref.shape, 1)
    ce = jnp.sum(jnp.where(lane == e, comb_ref[...], 0.0), axis=1, keepdims=True)
    acc_ref[...] += ce * y

    @pl.when(e == pl.num_programs(1) - 1)
    def _():
        x = x_ref[...] + gt_ref[...] * acc_ref[...]
        if final:
            x = x * lax.rsqrt(jnp.mean(x * x, axis=-1, keepdims=True) + NORM_EPS) * gf_ref[...]
        o_ref[...] = x


def _moe(h, x, gt, logits, br, wg, wu, wd, gf, tm, final):
    t, d = x.shape
    ne, _, de = wg.shape
    tok = pl.BlockSpec((tm, d), lambda i, e: (i, 0))
    full = lambda a: pl.BlockSpec(a.shape, lambda i, e: (0,) * a.ndim)
    return pl.pallas_call(
        functools.partial(_moe_kernel, final=final),
        grid=(t // tm, ne),
        in_specs=[tok, tok, _mod_spec(gt, t, tm), pl.BlockSpec((tm, LANES), lambda i, e: (i, 0)), full(br),
                  pl.BlockSpec((None, d, de), lambda i, e: (e, 0, 0)),
                  pl.BlockSpec((None, d, de), lambda i, e: (e, 0, 0)),
                  pl.BlockSpec((None, de, d), lambda i, e: (e, 0, 0)), full(gf)],
        out_specs=tok,
        out_shape=jax.ShapeDtypeStruct((t, d), F32),
        scratch_shapes=[pltpu.VMEM((tm, LANES), F32), pltpu.VMEM((tm, d), F32)],
        compiler_params=_cparams(("arbitrary", "arbitrary")),
        name="moe_ffn",
    )(h, x, gt, logits, br, wg, wu, wd, gf)


def _pad_heads(w, n_heads, width):
    lead = w.shape[:-1]
    w = w.reshape(lead + (n_heads, width))
    w = jnp.pad(w, [(0, 0)] * len(lead) + [(0, 0), (0, LANES - width)])
    return w.reshape(lead + (n_heads * LANES,))


def _rot_cols(w):
    half = MLA_ROPE // 2
    return jnp.concatenate([-w[..., half:], w[..., :half]], axis=-1)


def _in_weight(w):
    d = w.shape[0]
    o = 0
    parts = {}
    for name, width in (("fq", 512), ("fk", 128), ("fv", 128), ("ff", 8), ("sq", 512), ("sk", 128), ("sv", 128),
                        ("mdq", MLA_Q_LORA), ("mdkv", MLA_KV_LORA + MLA_ROPE), ("ga", d), ("gb", d), ("gc", d)):
        parts[name] = w[:, o:o + width]
        o += width
    kr = parts["mdkv"][:, MLA_KV_LORA:]
    misc = jnp.concatenate([kr, _rot_cols(kr), parts["ff"],
                            jnp.zeros((d, LANES - 2 * MLA_ROPE - N_HEADS), w.dtype)], axis=1)
    cols = [_pad_heads(parts["fq"], N_HEADS, HEAD_DIM), _pad_heads(parts["sq"], N_HEADS, HEAD_DIM),
            _pad_heads(parts["fk"], KV_HEADS, HEAD_DIM), _pad_heads(parts["fv"], KV_HEADS, HEAD_DIM),
            _pad_heads(parts["sk"], KV_HEADS, HEAD_DIM), _pad_heads(parts["sv"], KV_HEADS, HEAD_DIM),
            parts["ga"], parts["gb"], parts["gc"], parts["mdq"], parts["mdkv"][:, :MLA_KV_LORA], misc]
    return jnp.concatenate(cols, axis=1).astype(BF)


def _pad_rows(w, n_heads, width):
    d = w.shape[-1]
    w = w.reshape(n_heads, width, d)
    return jnp.pad(w, [(0, 0), (0, LANES - width), (0, 0)]).reshape(n_heads * LANES, d)


def _fox_select():
    sel = np.zeros((3, LANES, (N_HEADS + KV_HEADS) * LANES), np.float32)
    cst = np.zeros((1, (N_HEADS + KV_HEADS) * LANES), np.float32)
    for hd in range(N_HEADS):
        h, g = divmod(hd, GROUP)
        for p in range(3):
            sel[p, hd, hd * LANES + A_FQ + p] = 1.0
            sel[p, hd, (N_HEADS + h) * LANES + A_FK + 3 * g + p] = -1.0
            cst[0, hd * LANES + A_FK + 3 * g + p] = 1.0
    for h in range(KV_HEADS):
        for p in range(3):
            cst[0, (N_HEADS + h) * LANES + A_FQ + p] = 1.0
    return jnp.asarray(sel, BF), jnp.asarray(cst)


def _post_weights(b_forget, g_q, w_uq, g_kv, w_uk):
    uq = w_uq.reshape(MLA_Q_LORA, N_HEADS, MLA_QK)
    wn = uq[:, :, :MLA_NOPE].reshape(MLA_Q_LORA, N_HEADS * MLA_NOPE)
    wr = uq[:, :, MLA_NOPE:]
    wra = _pad_heads(wr.reshape(MLA_Q_LORA, N_HEADS * MLA_ROPE), N_HEADS, MLA_ROPE)
    wrb = _pad_heads(_rot_cols(wr).reshape(MLA_Q_LORA, N_HEADS * MLA_ROPE), N_HEADS, MLA_ROPE)
    eye = jnp.eye(N_HEADS, dtype=F32)
    wuk = jnp.einsum("lhn,hk->hnkl", w_uk, eye).reshape(N_HEADS * MLA_NOPE, N_HEADS * MLA_KV_LORA)
    sel, cst = _fox_select()
    bf = jnp.zeros((1, LANES), F32).at[0, M_FF:M_FF + N_HEADS].set(b_forget)
    return dict(bf=bf, gq=g_q.reshape(1, -1), gkv=g_kv.reshape(1, -1), wn=wn.astype(BF), wra=wra.astype(BF),
                wrb=wrb.astype(BF), wuk=wuk.astype(BF), sel=sel, cst=cst)


def _rope_table(pos):
    half = MLA_ROPE // 2
    inv = ROPE_THETA ** (-jnp.arange(half, dtype=F32) / half)
    ang = pos.astype(F32)[:, None] * inv[None, :]
    cos, sin = jnp.cos(ang), jnp.sin(ang)
    tab = jnp.concatenate([cos, cos, sin, sin], axis=1)
    return jnp.pad(tab, [(0, 0), (0, LANES - 2 * MLA_ROPE)])


def _tile(n, pref):
    return pref if n % pref == 0 else n


def kernel(x_prompt, x_sample, cache_fox_k, cache_fox_v, cache_fox_logf, cache_sb_k, cache_sb_v, cache_mla_ckv, cache_mla_kr, page_table, c_prompt, c_sample, w_ada, b_ada, g_mix, g_ffn, g_final, w_in, b_forget, g_mla_q, w_mla_uq, g_mla_kv, w_mla_uk, w_mla_uv, w_fox_o, w_sb_o, w_mla_o, w_o, w_router, b_router, w_exp_gate, w_exp_up, w_exp_down):
    bsz, seq, d = x_prompt.shape
    db, nt, _ = x_sample.shape
    depth = w_in.shape[0]
    n_pages = page_table.shape[1]
    page = cache_fox_k.shape[2]
    n_pool = cache_fox_k.shape[1]
    past = n_pages * page

    n_c = bsz + db
    c_all = jnp.pad(jnp.concatenate([c_prompt, c_sample], axis=0), [(0, (-n_c) % 8), (0, 0)])
    mod = _ada(c_all, w_ada, b_ada)

    w_in_r = [_in_weight(w_in[l]) for l in range(depth)]
    post_w = [_post_weights(b_forget[l], g_mla_q[l], w_mla_uq[l], g_mla_kv[l], w_mla_uk[l]) for l in range(depth)]
    wuv_pad = [jnp.pad(jnp.transpose(w_mla_uv[l], (1, 0, 2)), [(0, 0), (0, 0), (0, LANES - MLA_V)]).astype(BF)
               for l in range(depth)]
    wuv_cat = [w_mla_uv[l].reshape(MLA_KV_LORA, N_HEADS * MLA_V).astype(BF) for l in range(depth)]
    wf = [_pad_rows(w_fox_o[l], N_HEADS, HEAD_DIM).astype(BF) for l in range(depth)]
    ws = [_pad_rows(w_sb_o[l], N_HEADS, HEAD_DIM).astype(BF) for l in range(depth)]
    wm = [_pad_rows(w_mla_o[l], N_HEADS, MLA_V).astype(BF) for l in range(depth)]
    wo = w_o.astype(BF)
    wr_pad = jnp.pad(w_router, [(0, 0), (0, LANES - N_EXPERTS)])
    wr_hi = wr_pad.astype(BF)
    wr = jnp.stack([wr_hi, (wr_pad - wr_hi.astype(F32)).astype(BF)])
    br = jnp.pad(b_router, (0, LANES - N_EXPERTS)).reshape(1, LANES)
    wg, wu, wd = w_exp_gate.astype(BF), w_exp_up.astype(BF), w_exp_down.astype(BF)
    gfin = g_final.reshape(1, d)

    def tok_last(c):
        nd = c.ndim
        c = jnp.transpose(c, (0, 1) + tuple(range(3, nd)) + (2,))
        return c.reshape(c.shape[0], c.shape[1], -1, page)
    caches7 = (tok_last(cache_fox_k), tok_last(cache_fox_v), tok_last(cache_sb_k), tok_last(cache_sb_v),
               cache_mla_ckv, tok_last(cache_mla_kr), tok_last(cache_fox_logf))

    def mods(l, lo, n, rep):
        m = mod[l, lo:lo + n].reshape(n, 1, 6, d)
        if rep > 1:
            m = jnp.broadcast_to(m, (n, rep, 6, d)).reshape(1, n * rep, 6, d)
        return [m[:, :, k, :] for k in range(6)]

    def trunk(x, mod_lo, n_mod, rep, nb, s, pos, paged):
        t = nb * s
        tm = _tile(s, 512)
        tm_o = _tile(s, 256)
        tm_e = _tile(s, 1024)
        cs_tab = _rope_table(pos)
        rows = [[] for _ in range(7)]
        for l in range(depth):
            sh1, sc1, gt1, sh2, sc2, gt2 = mods(l, mod_lo, n_mod, rep)
            proj = _inproj(x, g_mix[l], sc1, sh1, w_in_r[l], tm)
            fq, sqq, kvb, mq, mk, lf, ckv, kr = _post(proj, cs_tab, post_w[l], nb, s, tm)
            kvf = proj[:, C_KV:C_KV + 4 * KV_HEADS * LANES].reshape(nb, s, 4, KV_HEADS, LANES)[..., :HEAD_DIM]
            for lst, r in zip(rows, (kvf[:, :, 0], kvf[:, :, 1], lf, kvf[:, :, 2], kvf[:, :, 3], ckv, kr)):
                lst.append(r)
            if not paged:
                tq = _tile(s, 256)
                o_f = _fox_attn(fq, kvb, tq, _tile(s, 512)).reshape(t, -1)
                o_s = _sb_attn(sqq, kvb, tq).reshape(t, -1)
                o_m = _mla_attn(mq, mk, wuv_pad[l], _tile(s, 128), _tile(s, 512)).reshape(t, -1)
            else:
                o_f, o_s, o_m = _decode(l, proj, fq, sqq, mq, kvf, lf, ckv, kr)
            x, h2, logits = _outproj(o_f, o_s, o_m, proj, x, gt1, sc2, sh2, g_ffn[l].reshape(1, d),
                                     wf[l], ws[l], wm[l], wo[l], wr, tm_o)
            x = _moe(h2, x, gt2, logits, br, wg[l], wu[l], wd[l], gfin, tm_e, l == depth - 1)
        return x, rows

    def _decode(l, proj, fq, sqq, mq, kvf, lf, ckv, kr):
        own = (jnp.arange(N_HEADS)[:, None] // GROUP == jnp.arange(KV_HEADS)[None, :])

        def q_bd(q):
            q = q.reshape(db, nt, N_HEADS, LANES)[..., :HEAD_DIM]
            q = jnp.where(own[None, None, :, :, None], q[:, :, :, None, :], jnp.zeros((), q.dtype))
            return q.reshape(db, nt * N_HEADS, KV_HEADS * HEAD_DIM)

        mq4 = mq.reshape(db, nt * N_HEADS, 2 * LANES)
        q4 = (q_bd(fq), q_bd(sqq), mq4[..., :LANES], mq4[..., LANES:LANES + MLA_ROPE])

        def new_t(r):
            r = r.reshape(db, nt, -1)
            return jnp.pad(jnp.transpose(r, (0, 2, 1)), [(0, 0), (0, 0), (0, page - nt)])

        ck_new = jnp.pad(ckv.reshape(db, nt, MLA_KV_LORA), [(0, 0), (0, page - nt), (0, 0)])
        kvr = kvf.reshape(db * nt, 4, KV_HEADS * HEAD_DIM)
        new7 = (new_t(kvr[:, 0]), new_t(kvr[:, 1]), new_t(kvr[:, 2]), new_t(kvr[:, 3]), ck_new,
                new_t(kr.reshape(db * nt, -1)), new_t(lf.reshape(db * nt, -1)))
        o_f, o_s, o_m = _decode_attn(page_table, q4, new7, caches7, wuv_cat[l], l, nt)

        def own_block(o, width):
            o = o.reshape(db, nt, N_HEADS, -1, width)
            idx = (jnp.arange(N_HEADS) // (N_HEADS // o.shape[3]))
            o = jnp.take_along_axis(o, idx[None, None, :, None, None], axis=3)[:, :, :, 0]
            o = jnp.pad(o, [(0, 0), (0, 0), (0, 0), (0, LANES - width)])
            return o.reshape(db * nt, N_HEADS * LANES).astype(BF)

        return own_block(o_f, HEAD_DIM), own_block(o_s, HEAD_DIM), own_block(o_m, MLA_V)

    y_p, rows_p = trunk(x_prompt.reshape(bsz * seq, d), 0, bsz, 1, bsz, seq, jnp.arange(seq), False)
    pos_s = jnp.tile(past + jnp.arange(nt), db)
    y_s, rows_s = trunk(x_sample.reshape(db * nt, d), bsz, db, nt, 1, db * nt, pos_s, True)

    def stack(rows, nb, s):
        shapes = [(KV_HEADS, HEAD_DIM), (KV_HEADS, HEAD_DIM), (N_HEADS,), (KV_HEADS, HEAD_DIM),
                  (KV_HEADS, HEAD_DIM), (MLA_KV_LORA,), (MLA_ROPE,)]
        return [jnp.stack([r.reshape((nb, s) + sh) for r in lst]) for lst, sh in zip(rows, shapes)]

    return (y_p.reshape(bsz, seq, d), y_s.reshape(db, nt, d), *stack(rows_p, bsz, seq), *stack(rows_s, db, nt))
```

```python
import functools

import jax
import jax.numpy as jnp
import numpy as np
from jax import lax
from jax.experimental import pallas as pl
from jax.experimental.pallas import tpu as pltpu

F32 = jnp.float32
BF = jnp.bfloat16

HEAD_DIM = 64
N_HEADS = 8
KV_HEADS = 2
GROUP = N_HEADS // KV_HEADS
MLA_NOPE = 64
MLA_ROPE = 32
MLA_QK = MLA_NOPE + MLA_ROPE
MLA_V = 64
MLA_Q_LORA = 256
MLA_KV_LORA = 128
ROPE_THETA = 10000.0
N_EXPERTS = 16
EXPERTS_PER_GROUP = 4
NORM_EPS = 1e-6
LANES = 128
NEG = -1e30
EXP_UNDERFLOW = -104.0
SB_DECODE_SUB = 1024

C_FQ, C_SQ, C_KV, C_GATE, C_MLA = 0, 1024, 2048, 3072, 6144
IN_W = 6656
M_KR, M_KRROT, M_FF = 0, 32, 64
A_FQ = 64
A_FK = 67

VMEM_LIMIT = 56 * 1024 * 1024


def _cparams(sem):
    return pltpu.CompilerParams(dimension_semantics=sem, vmem_limit_bytes=VMEM_LIMIT)


def _split3(x):
    hi = x.astype(BF)
    r1 = x - hi.astype(F32)
    mid = r1.astype(BF)
    lo = (r1 - mid.astype(F32)).astype(BF)
    return hi, mid, lo


def _softplus(z):
    return jnp.maximum(z, 0.0) + jnp.log1p(jnp.exp(-jnp.abs(z)))


def _ada_kernel(c_ref, w_ref, b_ref, o_ref):
    c = c_ref[...]
    a = (c * jax.nn.sigmoid(c)).astype(BF)
    o_ref[...] = jnp.dot(a, w_ref[...].astype(BF), preferred_element_type=F32) + b_ref[...]


def _ada(c_all, w_ada, b_ada):
    depth, d, n = w_ada.shape
    r = c_all.shape[0]
    tn = 1536
    return pl.pallas_call(
        _ada_kernel,
        grid=(depth, n // tn),
        in_specs=[pl.BlockSpec((r, d), lambda l, j: (0, 0)),
                  pl.BlockSpec((None, d, tn), lambda l, j: (l, 0, j)),
                  pl.BlockSpec((None, 1, tn), lambda l, j: (l, 0, j))],
        out_specs=pl.BlockSpec((None, r, tn), lambda l, j: (l, 0, j)),
        out_shape=jax.ShapeDtypeStruct((depth, r, n), F32),
        compiler_params=_cparams(("arbitrary", "arbitrary")),
        name="ada_mod",
    )(c_all, w_ada, b_ada.reshape(depth, 1, n))


def _inproj_kernel(x_ref, g_ref, sc_ref, sh_ref, w_ref, o_ref, h_ref):
    @pl.when(pl.program_id(1) == 0)
    def _():
        x = x_ref[...]
        y = x * lax.rsqrt(jnp.mean(x * x, axis=-1, keepdims=True) + NORM_EPS) * g_ref[...]
        h_ref[...] = (y * (1.0 + sc_ref[...]) + sh_ref[...]).astype(BF)

    o_ref[...] = jnp.dot(h_ref[...], w_ref[...], preferred_element_type=F32)


def _mod_spec(m, t, tm):
    nb, r, d = m.shape
    if r == 1:
        per = (t // tm) // nb
        return pl.BlockSpec((None, 1, d), lambda i, *_: (i // per, 0, 0))
    return pl.BlockSpec((None, tm, d), lambda i, *_: (0, i, 0))


def _inproj(x, g, sc, sh, w, tm):
    t, d = x.shape
    n = w.shape[1]
    tn = 512
    return pl.pallas_call(
        _inproj_kernel,
        grid=(t // tm, n // tn),
        in_specs=[pl.BlockSpec((tm, d), lambda i, j: (i, 0)),
                  pl.BlockSpec((1, d), lambda i, j: (0, 0)),
                  _mod_spec(sc, t, tm), _mod_spec(sh, t, tm),
                  pl.BlockSpec((d, tn), lambda i, j: (0, j))],
        out_specs=pl.BlockSpec((tm, tn), lambda i, j: (i, j)),
        out_shape=jax.ShapeDtypeStruct((t, n), F32),
        scratch_shapes=[pltpu.VMEM((tm, d), BF)],
        compiler_params=_cparams(("arbitrary", "arbitrary")),
        name="in_proj",
    )(x, g.reshape(1, d), sc, sh, w)


def _post_kernel(qf_ref, qs_ref, kv_ref, ml_ref, cs_ref, bf_ref, gq_ref, gkv_ref,
                 wn_ref, wra_ref, wrb_ref, wuk_ref, sel_ref, cst_ref,
                 fq_o, sq_o, kvb_o, mq_o, mk_o, lf_o, ckv_o, kr_o, carry_ref, *, mla_scale):
    tm = qf_ref.shape[0]

    @pl.when(pl.program_id(1) == 0)
    def _():
        carry_ref[...] = jnp.zeros_like(carry_ref)

    lane = lax.broadcasted_iota(jnp.int32, (tm, LANES), 1)
    misc = ml_ref[:, MLA_Q_LORA + MLA_KV_LORA:]
    cs = cs_ref[...]

    t1 = misc * cs
    kr = jnp.where(lane < MLA_ROPE, t1 + pltpu.roll(t1, LANES - M_KRROT, 1), 0.0)
    kr_o[...] = kr[:, :MLA_ROPE]

    lf = pltpu.roll(misc + bf_ref[...], LANES - M_FF, 1)
    lf = jnp.where(lane < N_HEADS, -_softplus(-lf), 0.0)
    lf_o[...] = lf[:, :N_HEADS]

    row = lax.broadcasted_iota(jnp.int32, (tm, tm), 0)
    col = lax.broadcasted_iota(jnp.int32, (tm, tm), 1)
    tri = (col <= row).astype(BF)
    hi, mid, lo = _split3(lf)
    f_cum = (jnp.dot(tri, hi, preferred_element_type=F32)
             + jnp.dot(tri, mid, preferred_element_type=F32)
             + jnp.dot(tri, lo, preferred_element_type=F32)) + carry_ref[...]
    carry_ref[...] = f_cum[tm - 1:tm, :]
    fh, fm, fl = _split3(f_cum)
    ext = (jnp.dot(fh, sel_ref[0], preferred_element_type=F32)
           + jnp.dot(fm, sel_ref[1], preferred_element_type=F32)
           + jnp.dot(fl, sel_ref[2], preferred_element_type=F32)) + cst_ref[...]

    scale = HEAD_DIM ** -0.5
    fq_o[...] = (qf_ref[...] * scale + ext[:, :N_HEADS * LANES]).astype(BF)
    sq_o[...] = (qs_ref[...] * scale).astype(BF)
    kv = kv_ref[...]
    kvb_o[:, :KV_HEADS * LANES] = (kv[:, :KV_HEADS * LANES] + ext[:, N_HEADS * LANES:]).astype(BF)
    kvb_o[:, KV_HEADS * LANES:] = kv[:, KV_HEADS * LANES:].astype(BF)

    mdq = ml_ref[:, :MLA_Q_LORA]
    cq = (mdq * lax.rsqrt(jnp.mean(mdq * mdq, axis=-1, keepdims=True) + NORM_EPS) * gq_ref[...]).astype(BF)
    nope = jnp.dot(cq, wn_ref[...], preferred_element_type=F32).astype(BF)
    q_lat = jnp.dot(nope, wuk_ref[...], preferred_element_type=F32)
    ra = jnp.dot(cq, wra_ref[...], preferred_element_type=F32)
    rb = jnp.dot(cq, wrb_ref[...], preferred_element_type=F32)
    cos_t = jnp.where(lane < MLA_ROPE, cs, 0.0)
    sin_t = jnp.where(lane < MLA_ROPE, pltpu.roll(cs, LANES - MLA_ROPE, 1), 0.0)
    for h in range(N_HEADS):
        sl = slice(h * LANES, (h + 1) * LANES)
        mq_o[:, 2 * h * LANES:(2 * h + 1) * LANES] = (q_lat[:, sl] * mla_scale).astype(BF)
        mq_o[:, (2 * h + 1) * LANES:(2 * h + 2) * LANES] = (
            (ra[:, sl] * cos_t + rb[:, sl] * sin_t) * mla_scale).astype(BF)

    craw = ml_ref[:, MLA_Q_LORA:MLA_Q_LORA + MLA_KV_LORA]
    ckv = craw * lax.rsqrt(jnp.mean(craw * craw, axis=-1, keepdims=True) + NORM_EPS) * gkv_ref[...]
    ckv_o[...] = ckv
    mk_o[:, :LANES] = ckv.astype(BF)
    mk_o[:, LANES:] = kr.astype(BF)


def _post(proj, cs_tab, pw, nb, s, tm):
    p3 = proj.reshape(nb, s, IN_W)
    full = lambda a: pl.BlockSpec(a.shape, lambda b, i: (0,) * a.ndim)
    seg = lambda w, c: pl.BlockSpec((None, tm, w), lambda b, i: (b, i, c // w))
    row = lambda w: pl.BlockSpec((None, tm, w), lambda b, i: (b, i, 0))
    outs = [(N_HEADS * LANES, BF), (N_HEADS * LANES, BF), (4 * KV_HEADS * LANES, BF),
            (N_HEADS * 2 * LANES, BF), (2 * LANES, BF), (N_HEADS, F32), (MLA_KV_LORA, F32), (MLA_ROPE, F32)]
    return pl.pallas_call(
        functools.partial(_post_kernel, mla_scale=MLA_QK ** -0.5),
        grid=(nb, s // tm),
        in_specs=[seg(1024, C_FQ), seg(1024, C_SQ), seg(1024, C_KV), seg(512, C_MLA),
                  pl.BlockSpec((tm, LANES), lambda b, i: (i, 0)),
                  full(pw["bf"]), full(pw["gq"]), full(pw["gkv"]), full(pw["wn"]), full(pw["wra"]),
                  full(pw["wrb"]), full(pw["wuk"]), full(pw["sel"]), full(pw["cst"])],
        out_specs=[row(w) for w, _ in outs],
        out_shape=[jax.ShapeDtypeStruct((nb, s, w), dt) for w, dt in outs],
        scratch_shapes=[pltpu.VMEM((1, LANES), F32)],
        compiler_params=_cparams(("arbitrary", "arbitrary")),
        name="post_proj",
    )(p3, p3, p3, p3, cs_tab, pw["bf"], pw["gq"], pw["gkv"], pw["wn"], pw["wra"], pw["wrb"],
      pw["wuk"], pw["sel"], pw["cst"])


def _stack_heads(q_ref, n, w):
    return jnp.concatenate([q_ref[:, g * w:(g + 1) * w] for g in range(n)], axis=0)


def _softmax_block(q, k, v, m_ref, l_ref, acc_ref, mask):
    s = lax.dot_general(q, k, (((1,), (1,)), ((), ())), preferred_element_type=F32)
    if mask is not None:
        s = jnp.where(mask, s, NEG)
    m_prev = m_ref[...]
    m_next = jnp.maximum(m_prev, jnp.max(s, axis=1)[:, None])
    p = jnp.exp(s - jnp.tile(m_next, (1, s.shape[1] // LANES)))
    alpha = jnp.exp(m_prev - m_next)
    l_ref[...] = alpha * l_ref[...] + jnp.sum(p, axis=1)[:, None]
    m_ref[...] = m_next
    acc_ref[...] = acc_ref[...] * alpha + jnp.dot(p.astype(BF), v, preferred_element_type=F32)


def _causal_mask(i, tq, j0, tk, n_rep, strict):
    qpos = i * tq + lax.broadcasted_iota(jnp.int32, (tq, tk), 0)
    kpos = j0 + lax.broadcasted_iota(jnp.int32, (tq, tk), 1)
    m = (kpos < qpos) if strict else (kpos <= qpos)
    return jnp.tile(m, (n_rep, 1))


def _fox_kernel(q_ref, k_ref, v_ref, o_ref, m_ref, l_ref, acc_ref, *, tq, tk):
    i = pl.program_id(2)
    q = _stack_heads(q_ref, GROUP, LANES)
    m_ref[...] = jnp.full_like(m_ref, NEG)
    l_ref[...] = jnp.zeros_like(l_ref)
    acc_ref[...] = jnp.zeros_like(acc_ref)
    n_full = (i * tq) // tk

    off = pl.multiple_of(n_full * tk, tk)
    _softmax_block(q, k_ref[pl.ds(off, tk), :], v_ref[pl.ds(off, tk), :], m_ref, l_ref, acc_ref,
                   _causal_mask(i, tq, off, tk, GROUP, False))

    def body(n, c):
        o2 = pl.multiple_of((n_full - 1 - n) * tk, tk)
        _softmax_block(q, k_ref[pl.ds(o2, tk), :], v_ref[pl.ds(o2, tk), :], m_ref, l_ref, acc_ref, None)
        return c

    lax.fori_loop(0, n_full, body, 0)
    out = acc_ref[...] / l_ref[...]
    for g in range(GROUP):
        o_ref[:, g * LANES:(g + 1) * LANES] = out[g * tq:(g + 1) * tq].astype(o_ref.dtype)


def _fox_attn(fq, kvb, tq, tk):
    b, s, _ = fq.shape
    m = GROUP * tq
    return pl.pallas_call(
        functools.partial(_fox_kernel, tq=tq, tk=tk),
        grid=(b, KV_HEADS, s // tq),
        in_specs=[pl.BlockSpec((None, tq, GROUP * LANES), lambda bb, h, i: (bb, i, h)),
                  pl.BlockSpec((None, s, LANES), lambda bb, h, i: (bb, 0, h)),
                  pl.BlockSpec((None, s, LANES), lambda bb, h, i: (bb, 0, KV_HEADS + h))],
        out_specs=pl.BlockSpec((None, tq, GROUP * LANES), lambda bb, h, i: (bb, i, h)),
        out_shape=jax.ShapeDtypeStruct((b, s, N_HEADS * LANES), BF),
        scratch_shapes=[pltpu.VMEM((m, LANES), F32), pltpu.VMEM((m, LANES), F32), pltpu.VMEM((m, LANES), F32)],
        compiler_params=_cparams(("arbitrary", "arbitrary", "arbitrary")),
        name="fox_attn",
    )(fq, kvb, kvb)


def _suffix_in_block(x, upper, pieces=3):
    parts = _split3(x)[:pieces]
    r = x.shape[0]
    res = jnp.dot(jnp.concatenate(parts, axis=0), upper, preferred_element_type=F32)
    out = res[:r]
    for p in range(1, pieces):
        out = out + res[p * r:(p + 1) * r]
    return out


def _sb_block(q, k, v, upper, r_ref, acc_ref, mask):
    z = lax.dot_general(q, k, (((1,), (1,)), ((), ())), preferred_element_type=F32)
    sp = _softplus(z)
    l1m = -sp if mask is None else jnp.where(mask, -sp, 0.0)
    later = _suffix_in_block(l1m, upper, pieces=2)
    e = z - sp + later + jnp.tile(r_ref[...], (1, z.shape[1] // LANES))
    a = jnp.exp(e) if mask is None else jnp.where(mask, jnp.exp(e), 0.0)
    acc_ref[...] += jnp.dot(a.astype(BF), v, preferred_element_type=F32)
    r_ref[...] += jnp.sum(l1m, axis=1)[:, None]


def _sb_kernel(q_ref, k_ref, v_ref, u_ref, o_ref, r_ref, acc_ref, *, tq):
    i = pl.program_id(2)
    q = _stack_heads(q_ref, GROUP, LANES)
    r_ref[...] = jnp.zeros_like(r_ref)
    acc_ref[...] = jnp.zeros_like(acc_ref)
    upper = u_ref[...]
    off = pl.multiple_of(i * tq, tq)
    _sb_block(q, k_ref[pl.ds(off, tq), :], v_ref[pl.ds(off, tq), :], upper, r_ref, acc_ref,
              _causal_mask(i, tq, off, tq, GROUP, True))

    def live():
        return jnp.max(r_ref[...]) > EXP_UNDERFLOW

    def body(c):
        n, _ = c
        o2 = pl.multiple_of((i - 1 - n) * tq, tq)
        _sb_block(q, k_ref[pl.ds(o2, tq), :], v_ref[pl.ds(o2, tq), :], upper, r_ref, acc_ref, None)
        return n + 1, live()

    lax.while_loop(lambda c: jnp.logical_and(c[0] < i, c[1]), body, (jnp.int32(0), live()))
    out = acc_ref[...]
    for g in range(GROUP):
        o_ref[:, g * LANES:(g + 1) * LANES] = out[g * tq:(g + 1) * tq].astype(o_ref.dtype)


def _upper(w):
    j = lax.broadcasted_iota(jnp.int32, (w, w), 0)
    s = lax.broadcasted_iota(jnp.int32, (w, w), 1)
    return (j > s).astype(BF)


def _sb_attn(sq, kvb, tq):
    b, s, _ = sq.shape
    m = GROUP * tq
    return pl.pallas_call(
        functools.partial(_sb_kernel, tq=tq),
        grid=(b, KV_HEADS, s // tq),
        in_specs=[pl.BlockSpec((None, tq, GROUP * LANES), lambda bb, h, i: (bb, i, h)),
                  pl.BlockSpec((None, s, LANES), lambda bb, h, i: (bb, 0, 2 * KV_HEADS + h)),
                  pl.BlockSpec((None, s, LANES), lambda bb, h, i: (bb, 0, 3 * KV_HEADS + h)),
                  pl.BlockSpec((tq, tq), lambda bb, h, i: (0, 0))],
        out_specs=pl.BlockSpec((None, tq, GROUP * LANES), lambda bb, h, i: (bb, i, h)),
        out_shape=jax.ShapeDtypeStruct((b, s, N_HEADS * LANES), BF),
        scratch_shapes=[pltpu.VMEM((m, LANES), F32), pltpu.VMEM((m, LANES), F32)],
        compiler_params=_cparams(("arbitrary", "arbitrary", "arbitrary")),
        name="sb_attn",
    )(sq, kvb, kvb, _upper(tq))


def _mla_kernel(q_ref, k_ref, wuv_ref, o_ref, m_ref, l_ref, acc_ref, *, tq, tk):
    i = pl.program_id(1)
    q = _stack_heads(q_ref, N_HEADS, 2 * LANES)
    m_ref[...] = jnp.full_like(m_ref, NEG)
    l_ref[...] = jnp.zeros_like(l_ref)
    acc_ref[...] = jnp.zeros_like(acc_ref)
    n_full = (i * tq) // tk

    def body(j, c):
        off = pl.multiple_of(j * tk, tk)
        kk = k_ref[pl.ds(off, tk), :]
        _softmax_block(q, kk, kk[:, :LANES], m_ref, l_ref, acc_ref, None)
        return c

    lax.fori_loop(0, n_full, body, 0)
    off = pl.multiple_of(n_full * tk, tk)
    kk = k_ref[pl.ds(off, tk), :]
    _softmax_block(q, kk, kk[:, :LANES], m_ref, l_ref, acc_ref, _causal_mask(i, tq, off, tk, N_HEADS, False))
    o_lat = (acc_ref[...] / l_ref[...]).astype(BF)
    for h in range(N_HEADS):
        o_ref[:, h * LANES:(h + 1) * LANES] = jnp.dot(
            o_lat[h * tq:(h + 1) * tq], wuv_ref[h], preferred_element_type=F32).astype(o_ref.dtype)


def _mla_attn(mq, mk, wuv, tq, tk):
    b, s, _ = mq.shape
    m = N_HEADS * tq
    return pl.pallas_call(
        functools.partial(_mla_kernel, tq=tq, tk=tk),
        grid=(b, s // tq),
        in_specs=[pl.BlockSpec((None, tq, N_HEADS * 2 * LANES), lambda bb, i: (bb, i, 0)),
                  pl.BlockSpec((None, s, 2 * LANES), lambda bb, i: (bb, 0, 0)),
                  pl.BlockSpec(wuv.shape, lambda bb, i: (0, 0, 0))],
        out_specs=pl.BlockSpec((None, tq, N_HEADS * LANES), lambda bb, i: (bb, i, 0)),
        out_shape=jax.ShapeDtypeStruct((b, s, N_HEADS * LANES), BF),
        scratch_shapes=[pltpu.VMEM((m, LANES), F32), pltpu.VMEM((m, LANES), F32), pltpu.VMEM((m, LANES), F32)],
        compiler_params=_cparams(("arbitrary", "arbitrary")),
        name="mla_attn",
    )(mq, mk, wuv)


def _suffix_lanes(x, upper, bw):
    w = x.shape[1]
    carry = jnp.zeros((x.shape[0], 1), F32)
    outs = [None] * (w // bw)
    for b in reversed(range(w // bw)):
        xb = x[:, b * bw:(b + 1) * bw]
        outs[b] = _suffix_in_block(xb, upper) + carry
        carry = carry + jnp.sum(xb, axis=1, keepdims=True)
    return (outs[0] if len(outs) == 1 else jnp.concatenate(outs, axis=1)), carry


def _online(s, v_nt, v_n, m_ref, l_ref, acc_ref):
    m_prev = m_ref[...]
    m_next = jnp.maximum(m_prev, jnp.max(s, axis=1, keepdims=True))
    p = jnp.exp(s - m_next)
    alpha = jnp.exp(m_prev - m_next)
    l_ref[...] = alpha * l_ref[...] + jnp.sum(p, axis=1, keepdims=True)
    m_ref[...] = m_next
    pb = p.astype(BF)
    if v_nt is not None:
        pv = lax.dot_general(pb, v_nt, (((1,), (1,)), ((), ())), preferred_element_type=F32)
    else:
        pv = jnp.dot(pb, v_n, preferred_element_type=F32)
    acc_ref[...] = acc_ref[...] * alpha + pv


def _decode_block(qf, qs, ql, qr, fk, fv, sk, sv, ck, kr, lf, upper, bw, st, new_mask, cn):
    (fm, fl, fa, sr, sa, mm, mlr, ma, rf) = st
    nt = qf.shape[0] // N_HEADS
    w = fk.shape[1]
    s = jnp.dot(qf, fk.astype(BF), preferred_element_type=F32)
    if new_mask is None:
        later, tot = _suffix_lanes(lf, upper, bw)
        bias = later + rf[...]
        s = jnp.concatenate([s[t * N_HEADS:(t + 1) * N_HEADS] + (bias + cn[:, t:t + 1]) for t in range(nt)], axis=0)
        rf[...] += tot
    else:
        le, strict = new_mask
        s = jnp.concatenate([s[t * N_HEADS:(t + 1) * N_HEADS] + (cn[:, t:t + 1] - cn) for t in range(nt)], axis=0)
        s = jnp.where(le, s, NEG)
    _online(s, fv.astype(BF), None, fm, fl, fa)
    def stick_breaking(lo, hi):
        z = jnp.dot(qs, sk[:, lo:hi].astype(BF), preferred_element_type=F32)
        sp = _softplus(z)
        l1m = -sp if new_mask is None else jnp.where(new_mask[1], -sp, 0.0)
        later, tot = _suffix_lanes(l1m, upper, bw)
        e = jnp.exp(z - sp + later + sr[...])
        a = e if new_mask is None else jnp.where(new_mask[1], e, 0.0)
        sa[...] += lax.dot_general(a.astype(BF), sv[:, lo:hi].astype(BF), (((1,), (1,)), ((), ())),
                                   preferred_element_type=F32)
        sr[...] += tot

    if new_mask is None:
        sub = min(SB_DECODE_SUB, w)
        for b in reversed(range(w // sub)):
            pl.when(jnp.max(sr[...]) > EXP_UNDERFLOW)(functools.partial(stick_breaking, b * sub, (b + 1) * sub))
    else:
        stick_breaking(0, w)
    ckb = ck.astype(BF)
    s = (lax.dot_general(ql, ckb, (((1,), (1,)), ((), ())), preferred_element_type=F32)
         + jnp.dot(qr, kr.astype(BF), preferred_element_type=F32))
    if new_mask is not None:
        s = jnp.where(new_mask[0], s, NEG)
    _online(s, None, ckb, mm, mlr, ma)


def _decode_kernel(pt_ref, qf_ref, qs_ref, ql_ref, qr_ref, fkn_ref, fvn_ref, skn_ref, svn_ref, ckn_ref,
                   krn_ref, lfn_ref, u_ref, wuv_ref,
                   fk_hbm, fv_hbm, sk_hbm, sv_hbm, ck_hbm, kr_hbm, lf_hbm,
                   of_ref, os_ref, om_ref,
                   fk_b, fv_b, sk_b, sv_b, ck_b, kr_b, lf_b, sem,
                   fm, fl, fa, sr, sa, mm, mlr, ma, rf, cn_ref, *, layer, ppc, nt):
    sq = pl.program_id(0)
    c = pl.program_id(1)
    n_seq = pl.num_programs(0)
    page = fk_b.shape[1]
    hbm = (fk_hbm, fv_hbm, sk_hbm, sv_hbm, ck_hbm, kr_hbm, lf_hbm)
    bufs = (fk_b, fv_b, sk_b, sv_b, ck_b, kr_b, lf_b)

    def copies(seq, first_page, slot):
        out = []
        for p in range(ppc):
            pg = pt_ref[seq, first_page + p]
            for a in range(7):
                if a == 4:
                    dst = bufs[a].at[slot, pl.ds(p * page, page), :]
                else:
                    dst = bufs[a].at[slot, :, pl.ds(p * page, page)]
                out.append(pltpu.make_async_copy(hbm[a].at[layer, pg], dst, sem.at[slot, a]))
        return out

    def start(seq, first_page, slot):
        for cp in copies(seq, first_page, slot):
            cp.start()

    def wait(slot):
        for cp in copies(0, 0, slot):
            cp.wait()

    @pl.when(jnp.logical_and(sq == 0, c == 0))
    def _():
        start(0, ppc, 0)

    qf, qs, ql, qr = qf_ref[...], qs_ref[...], ql_ref[...], qr_ref[...]
    upper = u_ref[...]
    st = (fm, fl, fa, sr, sa, mm, mlr, ma, rf)
    rows = nt * N_HEADS

    @pl.when(c == 0)
    def _():
        start(sq, 0, 1)
        for r in (fm, mm):
            r[...] = jnp.full_like(r, NEG)
        for r in (fl, fa, sr, sa, mlr, ma, rf):
            r[...] = jnp.zeros_like(r)
        lfn = lfn_ref[...]
        lane8 = lax.broadcasted_iota(jnp.int32, (N_HEADS, LANES), 1)
        later, tot = _suffix_lanes(jnp.where(lane8 < nt, lfn, 0.0), upper[:LANES, :LANES], LANES)
        cn = tot - later
        cn_ref[...] = cn
        trow = lax.broadcasted_iota(jnp.int32, (rows, LANES), 0) // N_HEADS
        kcol = lax.broadcasted_iota(jnp.int32, (rows, LANES), 1)
        _decode_block(qf, qs, ql, qr, fkn_ref[...], fvn_ref[...], skn_ref, svn_ref, ckn_ref[...],
                      krn_ref[...], None, upper[:LANES, :LANES], LANES, st, (kcol <= trow, kcol < trow), cn)
        wait(0)
        _decode_block(qf, qs, ql, qr, fk_b[0], fv_b[0], sk_b.at[0], sv_b.at[0], ck_b[0], kr_b[0], lf_b[0],
                      upper, upper.shape[0], st, None, cn)

    @pl.when(c == 1)
    def _():
        @pl.when(sq + 1 < n_seq)
        def _():
            start(sq + 1, ppc, 0)
        wait(1)
        _decode_block(qf, qs, ql, qr, fk_b[1], fv_b[1], sk_b.at[1], sv_b.at[1], ck_b[1], kr_b[1], lf_b[1],
                      upper, upper.shape[0], st, None, cn_ref[...])
        of_ref[...] = fa[...] / fl[...]
        os_ref[...] = sa[...]
        o_lat = (ma[...] / mlr[...]).astype(BF)
        om_ref[...] = jnp.dot(o_lat, wuv_ref[...], preferred_element_type=F32)


def _decode_attn(page_table, q4, new7, caches7, wuv_cat, layer, nt):
    qf, qs, ql, qr = q4
    db, rows, _ = qf.shape
    n_pages = page_table.shape[1]
    ppc = n_pages // 2
    page = caches7[0].shape[-1]
    w = ppc * page
    bw = min(256, w)
    upper = _upper(bw)
    per_seq = lambda a: pl.BlockSpec((None,) + a.shape[1:], lambda s, c, pt: (s,) + (0,) * (a.ndim - 1))
    full = lambda a: pl.BlockSpec(a.shape, lambda s, c, pt: (0,) * a.ndim)
    anyspec = pl.BlockSpec(memory_space=pl.ANY)
    out_w = (LANES, LANES, wuv_cat.shape[1])
    small = lambda wd: pltpu.VMEM((rows, wd), F32)
    grid_spec = pltpu.PrefetchScalarGridSpec(
        num_scalar_prefetch=1,
        grid=(db, 2),
        in_specs=[per_seq(a) for a in (qf, qs, ql, qr) + tuple(new7)] + [full(upper), full(wuv_cat)] + [anyspec] * 7,
        out_specs=[pl.BlockSpec((None, rows, wd), lambda s, c, pt: (s, 0, 0)) for wd in out_w],
        scratch_shapes=[pltpu.VMEM((2, LANES, w), F32)] * 4
        + [pltpu.VMEM((2, w, LANES), F32), pltpu.VMEM((2, MLA_ROPE, w), F32), pltpu.VMEM((2, N_HEADS, w), F32),
           pltpu.SemaphoreType.DMA((2, 7)),
           small(1), small(1), small(LANES), small(1), small(LANES), small(1), small(1), small(LANES),
           pltpu.VMEM((N_HEADS, 1), F32), pltpu.VMEM((N_HEADS, LANES), F32)],
    )
    return pl.pallas_call(
        functools.partial(_decode_kernel, layer=layer, ppc=ppc, nt=nt),
        grid_spec=grid_spec,
        out_shape=[jax.ShapeDtypeStruct((db, rows, wd), F32) for wd in out_w],
        compiler_params=_cparams(("arbitrary", "arbitrary")),
        name="decode_attn",
    )(page_table, qf, qs, ql, qr, *new7, upper, wuv_cat, *caches7)


def _outproj_kernel(of_ref, os_ref, om_ref, ga_ref, gb_ref, gc_ref, x_ref, gt_ref, sc_ref, sh_ref, g_ref,
                    wf_ref, ws_ref, wm_ref, wo_ref, wr_ref, xo_ref, ho_ref, lo_ref):
    yf = jnp.dot(of_ref[...], wf_ref[...], preferred_element_type=F32)
    ys = jnp.dot(os_ref[...], ws_ref[...], preferred_element_type=F32)
    ym = jnp.dot(om_ref[...], wm_ref[...], preferred_element_type=F32)
    merged = (jax.nn.sigmoid(ga_ref[...]) * yf + jax.nn.sigmoid(gb_ref[...]) * ys
              + jax.nn.sigmoid(gc_ref[...]) * ym)
    mix = jnp.dot(merged.astype(BF), wo_ref[...], preferred_element_type=F32)
    x = x_ref[...] + gt_ref[...] * mix
    xo_ref[...] = x
    y = x * lax.rsqrt(jnp.mean(x * x, axis=-1, keepdims=True) + NORM_EPS) * g_ref[...]
    h = y * (1.0 + sc_ref[...]) + sh_ref[...]
    hh = h.astype(BF)
    ho_ref[...] = hh
    hl = (h - hh.astype(F32)).astype(BF)
    lo_ref[...] = (jnp.dot(hh, wr_ref[0], preferred_element_type=F32)
                   + jnp.dot(hh, wr_ref[1], preferred_element_type=F32)
                   + jnp.dot(hl, wr_ref[0], preferred_element_type=F32))


def _outproj(o_f, o_s, o_m, proj, x, gt, sc, sh, g, wf, ws, wm, wo, wr, tm):
    t, d = x.shape
    tok = lambda w: pl.BlockSpec((tm, w), lambda i: (i, 0))
    gate = lambda k: pl.BlockSpec((tm, d), lambda i: (i, C_GATE // d + k))
    mod = _mod_spec(gt, t, tm)
    full = lambda a: pl.BlockSpec(a.shape, lambda i: (0,) * a.ndim)
    return pl.pallas_call(
        _outproj_kernel,
        grid=(t // tm,),
        in_specs=[tok(o_f.shape[1]), tok(o_s.shape[1]), tok(o_m.shape[1]), gate(0), gate(1), gate(2), tok(d),
                  mod, mod, mod, full(g), full(wf), full(ws), full(wm), full(wo), full(wr)],
        out_specs=[tok(d), tok(d), tok(LANES)],
        out_shape=[jax.ShapeDtypeStruct((t, d), F32), jax.ShapeDtypeStruct((t, d), BF),
                   jax.ShapeDtypeStruct((t, LANES), F32)],
        compiler_params=_cparams(("arbitrary",)),
        name="out_proj",
    )(o_f, o_s, o_m, proj, proj, proj, x, gt, sc, sh, g, wf, ws, wm, wo, wr)


def _route(logits, bias):
    tm = logits.shape[0]
    lane = lax.broadcasted_iota(jnp.int32, (tm, LANES), 1)
    scores = jax.nn.sigmoid(logits)
    sel = scores + bias
    pos = lane % EXPERTS_PER_GROUP
    grp = lane // EXPERTS_PER_GROUP
    n_groups = N_EXPERTS // EXPERTS_PER_GROUP

    def shifted(x, k):
        return pltpu.roll(x, (LANES - k) % LANES, 1)

    def beats(other, mine, k):
        return (other > mine) | ((other == mine) & (k < 0))

    rank = jnp.zeros((tm, LANES), jnp.int32)
    for k in range(-EXPERTS_PER_GROUP + 1, EXPERTS_PER_GROUP):
        if k == 0:
            continue
        same = (pos + k >= 0) & (pos + k < EXPERTS_PER_GROUP)
        rank += (same & beats(shifted(sel, k), sel, k)).astype(jnp.int32)
    top2 = jnp.where(rank < 2, sel, 0.0)
    gscore = top2
    for k in range(-EXPERTS_PER_GROUP + 1, EXPERTS_PER_GROUP):
        if k == 0:
            continue
        same = (pos + k >= 0) & (pos + k < EXPERTS_PER_GROUP)
        gscore += jnp.where(same, shifted(top2, k), 0.0)
    grank = jnp.zeros((tm, LANES), jnp.int32)
    for k in range(-n_groups + 1, n_groups):
        if k == 0:
            continue
        ok = (grp + k >= 0) & (grp + k < n_groups)
        grank += (ok & beats(shifted(gscore, k * EXPERTS_PER_GROUP), gscore, k)).astype(jnp.int32)
    chosen = (grank == 0) & (rank < 2) & (lane < N_EXPERTS)
    wts = jnp.where(chosen, scores, 0.0)
    return wts / jnp.sum(wts, axis=1, keepdims=True)


def _moe_kernel(h_ref, x_ref, gt_ref, lg_ref, br_ref, wg_ref, wu_ref, wd_ref, gf_ref, o_ref,
                comb_ref, acc_ref, *, final):
    e = pl.program_id(1)
    h = h_ref[...]

    @pl.when(e == 0)
    def _():
        comb_ref[...] = _route(lg_ref[...], br_ref[...])
        acc_ref[...] = jnp.zeros_like(acc_ref)

    a = jnp.dot(h, wg_ref[...], preferred_element_type=F32)
    u = jnp.dot(h, wu_ref[...], preferred_element_type=F32)
    hid = (a * jax.nn.sigmoid(a) * u).astype(BF)
    y = jnp.dot(hid, wd_ref[...], preferred_element_type=F32)
    lane = lax.broadcasted_iota(jnp.int32, comb_ref.shape, 1)
    ce = jnp.sum(jnp.where(lane == e, comb_ref[...], 0.0), axis=1, keepdims=True)
    acc_ref[...] += ce * y

    @pl.when(e == pl.num_programs(1) - 1)
    def _():
        x = x_ref[...] + gt_ref[...] * acc_ref[...]
        if final:
            x = x * lax.rsqrt(jnp.mean(x * x, axis=-1, keepdims=True) + NORM_EPS) * gf_ref[...]
        o_ref[...] = x


def _moe(h, x, gt, logits, br, wg, wu, wd, gf, tm, final):
    t, d = x.shape
    ne, _, de = wg.shape
    tok = pl.BlockSpec((tm, d), lambda i, e: (i, 0))
    full = lambda a: pl.BlockSpec(a.shape, lambda i, e: (0,) * a.ndim)
    return pl.pallas_call(
        functools.partial(_moe_kernel, final=final),
        grid=(t // tm, ne),
        in_specs=[tok, tok, _mod_spec(gt, t, tm), pl.BlockSpec((tm, LANES), lambda i, e: (i, 0)), full(br),
                  pl.BlockSpec((None, d, de), lambda i, e: (e, 0, 0)),
                  pl.BlockSpec((None, d, de), lambda i, e: (e, 0, 0)),
                  pl.BlockSpec((None, de, d), lambda i, e: (e, 0, 0)), full(gf)],
        out_specs=tok,
        out_shape=jax.ShapeDtypeStruct((t, d), F32),
        scratch_shapes=[pltpu.VMEM((tm, LANES), F32), pltpu.VMEM((tm, d), F32)],
        compiler_params=_cparams(("arbitrary", "arbitrary")),
        name="moe_ffn",
    )(h, x, gt, logits, br, wg, wu, wd, gf)


def _pad_heads(w, n_heads, width):
    lead = w.shape[:-1]
    w = w.reshape(lead + (n_heads, width))
    w = jnp.pad(w, [(0, 0)] * len(lead) + [(0, 0), (0, LANES - width)])
    return w.reshape(lead + (n_heads * LANES,))


def _rot_cols(w):
    half = MLA_ROPE // 2
    return jnp.concatenate([-w[..., half:], w[..., :half]], axis=-1)


def _in_weight(w):
    d = w.shape[0]
    o = 0
    parts = {}
    for name, width in (("fq", 512), ("fk", 128), ("fv", 128), ("ff", 8), ("sq", 512), ("sk", 128), ("sv", 128),
                        ("mdq", MLA_Q_LORA), ("mdkv", MLA_KV_LORA + MLA_ROPE), ("ga", d), ("gb", d), ("gc", d)):
        parts[name] = w[:, o:o + width]
        o += width
    kr = parts["mdkv"][:, MLA_KV_LORA:]
    misc = jnp.concatenate([kr, _rot_cols(kr), parts["ff"],
                            jnp.zeros((d, LANES - 2 * MLA_ROPE - N_HEADS), w.dtype)], axis=1)
    cols = [_pad_heads(parts["fq"], N_HEADS, HEAD_DIM), _pad_heads(parts["sq"], N_HEADS, HEAD_DIM),
            _pad_heads(parts["fk"], KV_HEADS, HEAD_DIM), _pad_heads(parts["fv"], KV_HEADS, HEAD_DIM),
            _pad_heads(parts["sk"], KV_HEADS, HEAD_DIM), _pad_heads(parts["sv"], KV_HEADS, HEAD_DIM),
            parts["ga"], parts["gb"], parts["gc"], parts["mdq"], parts["mdkv"][:, :MLA_KV_LORA], misc]
    return jnp.concatenate(cols, axis=1).astype(BF)


def _pad_rows(w, n_heads, width):
    d = w.shape[-1]
    w = w.reshape(n_heads, width, d)
    return jnp.pad(w, [(0, 0), (0, LANES - width), (0, 0)]).reshape(n_heads * LANES, d)


def _fox_select():
    sel = np.zeros((3, LANES, (N_HEADS + KV_HEADS) * LANES), np.float32)
    cst = np.zeros((1, (N_HEADS + KV_HEADS) * LANES), np.float32)
    for hd in range(N_HEADS):
        h, g = divmod(hd, GROUP)
        for p in range(3):
            sel[p, hd, hd * LANES + A_FQ + p] = 1.0
            sel[p, hd, (N_HEADS + h) * LANES + A_FK + 3 * g + p] = -1.0
            cst[0, hd * LANES + A_FK + 3 * g + p] = 1.0
    for h in range(KV_HEADS):
        for p in range(3):
            cst[0, (N_HEADS + h) * LANES + A_FQ + p] = 1.0
    return jnp.asarray(sel, BF), jnp.asarray(cst)


def _post_weights(b_forget, g_q, w_uq, g_kv, w_uk):
    uq = w_uq.reshape(MLA_Q_LORA, N_HEADS, MLA_QK)
    wn = uq[:, :, :MLA_NOPE].reshape(MLA_Q_LORA, N_HEADS * MLA_NOPE)
    wr = uq[:, :, MLA_NOPE:]
    wra = _pad_heads(wr.reshape(MLA_Q_LORA, N_HEADS * MLA_ROPE), N_HEADS, MLA_ROPE)
    wrb = _pad_heads(_rot_cols(wr).reshape(MLA_Q_LORA, N_HEADS * MLA_ROPE), N_HEADS, MLA_ROPE)
    eye = jnp.eye(N_HEADS, dtype=F32)
    wuk = jnp.einsum("lhn,hk->hnkl", w_uk, eye).reshape(N_HEADS * MLA_NOPE, N_HEADS * MLA_KV_LORA)
    sel, cst = _fox_select()
    bf = jnp.zeros((1, LANES), F32).at[0, M_FF:M_FF + N_HEADS].set(b_forget)
    return dict(bf=bf, gq=g_q.reshape(1, -1), gkv=g_kv.reshape(1, -1), wn=wn.astype(BF), wra=wra.astype(BF),
                wrb=wrb.astype(BF), wuk=wuk.astype(BF), sel=sel, cst=cst)


def _rope_table(pos):
    half = MLA_ROPE // 2
    inv = ROPE_THETA ** (-jnp.arange(half, dtype=F32) / half)
    ang = pos.astype(F32)[:, None] * inv[None, :]
    cos, sin = jnp.cos(ang), jnp.sin(ang)
    tab = jnp.concatenate([cos, cos, sin, sin], axis=1)
    return jnp.pad(tab, [(0, 0), (0, LANES - 2 * MLA_ROPE)])


def _tile(n, pref):
    return pref if n % pref == 0 else n


def kernel(x_prompt, x_sample, cache_fox_k, cache_fox_v, cache_fox_logf, cache_sb_k, cache_sb_v, cache_mla_ckv, cache_mla_kr, page_table, c_prompt, c_sample, w_ada, b_ada, g_mix, g_ffn, g_final, w_in, b_forget, g_mla_q, w_mla_uq, g_mla_kv, w_mla_uk, w_mla_uv, w_fox_o, w_sb_o, w_mla_o, w_o, w_router, b_router, w_exp_gate, w_exp_up, w_exp_down):
    bsz, seq, d = x_prompt.shape
    db, nt, _ = x_sample.shape
    depth = w_in.shape[0]
    n_pages = page_table.shape[1]
    page = cache_fox_k.shape[2]
    n_pool = cache_fox_k.shape[1]
    past = n_pages * page

    n_c = bsz + db
    c_all = jnp.pad(jnp.concatenate([c_prompt, c_sample], axis=0), [(0, (-n_c) % 8), (0, 0)])
    mod = _ada(c_all, w_ada, b_ada)

    w_in_r = [_in_weight(w_in[l]) for l in range(depth)]
    post_w = [_post_weights(b_forget[l], g_mla_q[l], w_mla_uq[l], g_mla_kv[l], w_mla_uk[l]) for l in range(depth)]
    wuv_pad = [jnp.pad(jnp.transpose(w_mla_uv[l], (1, 0, 2)), [(0, 0), (0, 0), (0, LANES - MLA_V)]).astype(BF)
               for l in range(depth)]
    wuv_cat = [w_mla_uv[l].reshape(MLA_KV_LORA, N_HEADS * MLA_V).astype(BF) for l in range(depth)]
    wf = [_pad_rows(w_fox_o[l], N_HEADS, HEAD_DIM).astype(BF) for l in range(depth)]
    ws = [_pad_rows(w_sb_o[l], N_HEADS, HEAD_DIM).astype(BF) for l in range(depth)]
    wm = [_pad_rows(w_mla_o[l], N_HEADS, MLA_V).astype(BF) for l in range(depth)]
    wo = w_o.astype(BF)
    wr_pad = jnp.pad(w_router, [(0, 0), (0, LANES - N_EXPERTS)])
    wr_hi = wr_pad.astype(BF)
    wr = jnp.stack([wr_hi, (wr_pad - wr_hi.astype(F32)).astype(BF)])
    br = jnp.pad(b_router, (0, LANES - N_EXPERTS)).reshape(1, LANES)
    wg, wu, wd = w_exp_gate.astype(BF), w_exp_up.astype(BF), w_exp_down.astype(BF)
    gfin = g_final.reshape(1, d)

    def tok_last(c):
        nd = c.ndim
        c = jnp.transpose(c, (0, 1) + tuple(range(3, nd)) + (2,))
        return c.reshape(c.shape[0], c.shape[1], -1, page)
    caches7 = (tok_last(cache_fox_k), tok_last(cache_fox_v), tok_last(cache_sb_k), tok_last(cache_sb_v),
               cache_mla_ckv, tok_last(cache_mla_kr), tok_last(cache_fox_logf))

    def mods(l, lo, n, rep):
        m = mod[l, lo:lo + n].reshape(n, 1, 6, d)
        if rep > 1:
            m = jnp.broadcast_to(m, (n, rep, 6, d)).reshape(1, n * rep, 6, d)
        return [m[:, :, k, :] for k in range(6)]

    def trunk(x, mod_lo, n_mod, rep, nb, s, pos, paged):
        t = nb * s
        tm = _tile(s, 512)
        tm_o = _tile(s, 256)
        tm_e = _tile(s, 1024)
        cs_tab = _rope_table(pos)
        rows = [[] for _ in range(7)]
        for l in range(depth):
            sh1, sc1, gt1, sh2, sc2, gt2 = mods(l, mod_lo, n_mod, rep)
            proj = _inproj(x, g_mix[l], sc1, sh1, w_in_r[l], _tile(s, 1024))
            fq, sqq, kvb, mq, mk, lf, ckv, kr = _post(proj, cs_tab, post_w[l], nb, s, tm)
            kvf = proj[:, C_KV:C_KV + 4 * KV_HEADS * LANES].reshape(nb, s, 4, KV_HEADS, LANES)[..., :HEAD_DIM]
            for lst, r in zip(rows, (kvf[:, :, 0], kvf[:, :, 1], lf, kvf[:, :, 2], kvf[:, :, 3], ckv, kr)):
                lst.append(r)
            if not paged:
                tq = _tile(s, 256)
                o_f = _fox_attn(fq, kvb, tq, _tile(s, 512)).reshape(t, -1)
                o_s = _sb_attn(sqq, kvb, tq).reshape(t, -1)
                o_m = _mla_attn(mq, mk, wuv_pad[l], _tile(s, 128), _tile(s, 512)).reshape(t, -1)
            else:
                o_f, o_s, o_m = _decode(l, proj, fq, sqq, mq, kvf, lf, ckv, kr)
            x, h2, logits = _outproj(o_f, o_s, o_m, proj, x, gt1, sc2, sh2, g_ffn[l].reshape(1, d),
                                     wf[l], ws[l], wm[l], wo[l], wr, tm_o)
            x = _moe(h2, x, gt2, logits, br, wg[l], wu[l], wd[l], gfin, tm_e, l == depth - 1)
        return x, rows

    def _decode(l, proj, fq, sqq, mq, kvf, lf, ckv, kr):
        own = (jnp.arange(N_HEADS)[:, None] // GROUP == jnp.arange(KV_HEADS)[None, :])

        def q_bd(q):
            q = q.reshape(db, nt, N_HEADS, LANES)[..., :HEAD_DIM]
            q = jnp.where(own[None, None, :, :, None], q[:, :, :, None, :], jnp.zeros((), q.dtype))
            return q.reshape(db, nt * N_HEADS, KV_HEADS * HEAD_DIM)

        mq4 = mq.reshape(db, nt * N_HEADS, 2 * LANES)
        q4 = (q_bd(fq), q_bd(sqq), mq4[..., :LANES], mq4[..., LANES:LANES + MLA_ROPE])

        def new_t(r):
            r = r.reshape(db, nt, -1)
            return jnp.pad(jnp.transpose(r, (0, 2, 1)), [(0, 0), (0, 0), (0, page - nt)])

        ck_new = jnp.pad(ckv.reshape(db, nt, MLA_KV_LORA), [(0, 0), (0, page - nt), (0, 0)])
        kvr = kvf.reshape(db * nt, 4, KV_HEADS * HEAD_DIM)
        new7 = (new_t(kvr[:, 0]), new_t(kvr[:, 1]), new_t(kvr[:, 2]), new_t(kvr[:, 3]), ck_new,
                new_t(kr.reshape(db * nt, -1)), new_t(lf.reshape(db * nt, -1)))
        o_f, o_s, o_m = _decode_attn(page_table, q4, new7, caches7, wuv_cat[l], l, nt)

        def own_block(o, width):
            o = o.reshape(db, nt, N_HEADS, -1, width)
            idx = (jnp.arange(N_HEADS) // (N_HEADS // o.shape[3]))
            o = jnp.take_along_axis(o, idx[None, None, :, None, None], axis=3)[:, :, :, 0]
            o = jnp.pad(o, [(0, 0), (0, 0), (0, 0), (0, LANES - width)])
            return o.reshape(db * nt, N_HEADS * LANES).astype(BF)

        return own_block(o_f, HEAD_DIM), own_block(o_s, HEAD_DIM), own_block(o_m, MLA_V)

    y_p, rows_p = trunk(x_prompt.reshape(bsz * seq, d), 0, bsz, 1, bsz, seq, jnp.arange(seq), False)
    pos_s = jnp.tile(past + jnp.arange(nt), db)
    y_s, rows_s = trunk(x_sample.reshape(db * nt, d), bsz, db, nt, 1, db * nt, pos_s, True)

    def stack(rows, nb, s):
        shapes = [(KV_HEADS, HEAD_DIM), (KV_HEADS, HEAD_DIM), (N_HEADS,), (KV_HEADS, HEAD_DIM),
                  (KV_HEADS, HEAD_DIM), (MLA_KV_LORA,), (MLA_ROPE,)]
        return [jnp.stack([r.reshape((nb, s) + sh) for r in lst]) for lst, sh in zip(rows, shapes)]

    return (y_p.reshape(bsz, seq, d), y_s.reshape(db, nt, d), *stack(rows_p, bsz, seq), *stack(rows_s, db, nt))
```
